```python
import jax, jax.numpy as jnp
from jax import lax
import numpy as np

D_MODEL = 1024
BATCH = 32
SEQ = 2048
DEPTH = 4

GRID_W = 64
CTX_LEN = 256
EPS = 1e-6
GLA_HEADS = 4
GLA_DV = D_MODEL // 2 // GLA_HEADS
GLA_DK = GLA_DV // 2
GLA_RANK = 16
GLA_NORMALIZER = 16.0
GLA_CHUNK = 64
CONV_CH = D_MODEL // 2
CONV_WIDTH = 31
ATT_HEAD_DIM = 128
ATT_HEADS = D_MODEL // ATT_HEAD_DIM
ATT_KV_HEADS = max(1, ATT_HEADS // 4)
ATT_BLOCK = 128
ROPE_THETA = 10000.0
MOE_GROUPS = 4
MOE_EXPERTS_PER_GROUP = 4
MOE_EXPERTS = MOE_GROUPS * MOE_EXPERTS_PER_GROUP
MOE_TOP_K = 2
MOE_HIDDEN = D_MODEL // 2
N_EVEN = (DEPTH + 1) // 2
N_ODD = DEPTH // 2
EVEN_SPLITS = [GLA_HEADS * GLA_DK, GLA_HEADS * GLA_DK, GLA_HEADS * GLA_DV, GLA_HEADS * GLA_DV,
               GLA_RANK, GLA_RANK, CONV_CH, CONV_CH]
EVEN_IN = int(sum(EVEN_SPLITS))
EVEN_MIX = GLA_HEADS * GLA_DV + CONV_CH
ODD_IN = (ATT_HEADS + 2 * ATT_KV_HEADS) * ATT_HEAD_DIM
ODD_MIX = ATT_HEADS * ATT_HEAD_DIM

kernel_name = "hybrid_gla_conv_gqa_hmoe_dit"


def rms_norm(x):
    xf = x.astype(jnp.float32)
    return (xf * lax.rsqrt(jnp.mean(xf * xf, axis=-1, keepdims=True) + EPS)).astype(x.dtype)


def layer_norm(x, g, b):
    xf = x.astype(jnp.float32)
    mu = jnp.mean(xf, axis=-1, keepdims=True)
    var = jnp.mean(jnp.square(xf - mu), axis=-1, keepdims=True)
    return ((xf - mu) * lax.rsqrt(var + EPS)).astype(x.dtype) * g + b


def modulate(x, shift, scale):
    return rms_norm(x) * (1.0 + scale) + shift


def gla_chunked(q, k, v, log_a, s0):
    b_, h_, t_, dk = q.shape
    dv = v.shape[-1]
    n = t_ // GLA_CHUNK
    qc = q.reshape(b_, h_, n, GLA_CHUNK, dk)
    kc = k.reshape(b_, h_, n, GLA_CHUNK, dk)
    vc = v.reshape(b_, h_, n, GLA_CHUNK, dv)
    cum = jnp.cumsum(log_a.astype(jnp.float32).reshape(b_, h_, n, GLA_CHUNK, dk), axis=3)
    last = cum[:, :, :, -1:, :]
    q_t = qc * jnp.exp(cum)
    k_t = kc * jnp.exp(-cum)
    k_end = kc * jnp.exp(last - cum)
    mask = jnp.tril(jnp.ones((GLA_CHUNK, GLA_CHUNK), dtype=bool))
    att = jnp.where(mask, jnp.einsum('bhnid,bhnjd->bhnij', q_t, k_t), 0.0)
    o_intra = jnp.einsum('bhnij,bhnjv->bhniv', att, vc)
    u = jnp.einsum('bhnjd,bhnjv->bhndv', k_end, vc).astype(jnp.float32)
    chunk_decay = jnp.exp(last[:, :, :, 0, :])

    def step(s, inp):
        dec, uu = inp
        return dec[..., None] * s + uu, s

    s_final, s_prev = lax.scan(step, s0, (jnp.moveaxis(chunk_decay, 2, 0), jnp.moveaxis(u, 2, 0)))
    s_prev = jnp.moveaxis(s_prev, 0, 2)
    o_inter = jnp.einsum('bhnid,bhndv->bhniv', q_t, s_prev)
    o = (o_intra + o_inter).reshape(b_, h_, t_, dv).astype(v.dtype)
    return o, s_final


def even_mixer(h, s0_f, s0_b, w_in, dec_w_f, dec_b_f, dec_w_b, dec_b_b, gla_norm_g,
               conv_w, conv_b, conv_norm_g, conv_norm_b, w_out):
    b_, t_, _ = h.shape
    p = h @ w_in
    q, k, v, r, zf, zb, ca, cg = jnp.split(p, np.cumsum(EVEN_SPLITS)[:-1].tolist(), axis=-1)

    def heads(y, d):
        return y.reshape(b_, t_, GLA_HEADS, d).transpose(0, 2, 1, 3)

    q = heads(q, GLA_DK) * (GLA_DK ** -0.5)
    k = heads(k, GLA_DK)
    v = heads(v, GLA_DV)
    la_f = heads(jax.nn.log_sigmoid((zf @ dec_w_f + dec_b_f).astype(jnp.float32)) / GLA_NORMALIZER, GLA_DK)
    la_b = heads(jax.nn.log_sigmoid((zb @ dec_w_b + dec_b_b).astype(jnp.float32)) / GLA_NORMALIZER, GLA_DK)
    o_f, s_f = gla_chunked(q, k, v, la_f, s0_f)
    flip = lambda y: jnp.flip(y, axis=2)
    o_b, s_b = gla_chunked(flip(q), flip(k), flip(v), flip(la_b), s0_b)
    o = rms_norm(o_f + flip(o_b)) * gla_norm_g
    o = o.transpose(0, 2, 1, 3).reshape(b_, t_, GLA_HEADS * GLA_DV) * jax.nn.silu(r)
    u = ca * jax.nn.sigmoid(cg)
    u = lax.conv_general_dilated(u, conv_w, window_strides=(1,),
                                 padding=[(CONV_WIDTH // 2, CONV_WIDTH // 2)],
                                 dimension_numbers=('NWC', 'WIO', 'NWC'),
                                 feature_group_count=CONV_CH) + conv_b
    u = jax.nn.silu(layer_norm(u, conv_norm_g, conv_norm_b))
    out = jnp.concatenate([o, u.astype(o.dtype)], axis=-1) @ w_out
    return out, s_f, s_b


def axial_rope_tables(t_len):
    t = jnp.arange(t_len)
    row = (t // GRID_W).astype(jnp.float32)
    col = (t % GRID_W).astype(jnp.float32)
    pairs_per_axis = ATT_HEAD_DIM // 4
    inv = ROPE_THETA ** (-jnp.arange(pairs_per_axis, dtype=jnp.float32) / pairs_per_axis)
    ang = jnp.concatenate([row[:, None] * inv, col[:, None] * inv], axis=-1)
    return jnp.cos(ang), jnp.sin(ang)


def apply_rope(x, cos, sin):
    xf = x.astype(jnp.float32).reshape(*x.shape[:-1], -1, 2)
    x0, x1 = xf[..., 0], xf[..., 1]
    out = jnp.stack([x0 * cos - x1 * sin, x0 * sin + x1 * cos], axis=-1).reshape(x.shape)
    return out.astype(x.dtype)


def attn_project(h, w_in, q_norm_g, k_norm_g):
    b_, t_, _ = h.shape
    q, k, v = jnp.split(h @ w_in, [ATT_HEADS * ATT_HEAD_DIM, (ATT_HEADS + ATT_KV_HEADS) * ATT_HEAD_DIM], axis=-1)
    q = rms_norm(q.reshape(b_, t_, ATT_HEADS, ATT_HEAD_DIM)) * q_norm_g
    k = rms_norm(k.reshape(b_, t_, ATT_KV_HEADS, ATT_HEAD_DIM)) * k_norm_g
    v = v.reshape(b_, t_, ATT_KV_HEADS, ATT_HEAD_DIM)
    return q.transpose(0, 2, 1, 3), k.transpose(0, 2, 1, 3), v.transpose(0, 2, 1, 3)


def block_attention(q, k, v):
    b_, hq, tq, hd = q.shape
    kvh = k.shape[1]
    g = hq // kvh
    nb = tq // ATT_BLOCK
    qb = q.reshape(b_, kvh, g, nb, ATT_BLOCK, hd).transpose(3, 0, 1, 2, 4, 5)
    scale = hd ** -0.5

    def one_block(qi):
        s = jnp.einsum('bkgqd,bksd->bkgqs', qi, k).astype(jnp.float32) * scale
        p = jax.nn.softmax(s, axis=-1).astype(v.dtype)
        return jnp.einsum('bkgqs,bksd->bkgqd', p, v)

    o = lax.map(one_block, qb)
    return o.transpose(1, 0, 4, 2, 3, 5).reshape(b_, tq, hq * hd)


def odd_mixer(h_lat, h_ctx, cos, sin, w_in, q_norm_g, k_norm_g, w_out, need_ctx):
    q_c, k_c, v_c = attn_project(h_ctx, w_in, q_norm_g, k_norm_g)
    q_l, k_l, v_l = attn_project(h_lat, w_in, q_norm_g, k_norm_g)
    q_l = apply_rope(q_l, cos, sin)
    k_l = apply_rope(k_l, cos, sin)
    k_all = jnp.concatenate([k_c, k_l], axis=2)
    v_all = jnp.concatenate([v_c, v_l], axis=2)
    o_lat = block_attention(q_l, k_all, v_all) @ w_out
    o_ctx = block_attention(q_c, k_c, v_c) @ w_out if need_ctx else None
    return o_lat, o_ctx


def hier_moe(x, w_group, b_group, w_router, b_router, w_gate, w_up, w_down):
    g_logits = (x @ w_group).astype(jnp.float32) + b_group
    g_prob = jax.nn.softmax(g_logits, axis=-1)
    _, g_idx = lax.top_k(g_logits, 1)
    p_group = jnp.take_along_axis(g_prob, g_idx, axis=-1)
    e_all = jnp.einsum('nd,dge->nge', x, w_router).astype(jnp.float32) + b_router
    e_logits = jnp.einsum('ng,nge->ne', jax.nn.one_hot(g_idx[:, 0], MOE_GROUPS, dtype=jnp.float32), e_all)
    e_prob = jax.nn.softmax(e_logits, axis=-1)
    top_p, top_i = lax.top_k(e_prob, MOE_TOP_K)
    weights = p_group * top_p / jnp.sum(top_p, axis=-1, keepdims=True)
    expert_id = g_idx * MOE_EXPERTS_PER_GROUP + top_i
    combine = jnp.einsum('nk,nke->ne', weights, jax.nn.one_hot(expert_id, MOE_EXPERTS, dtype=jnp.float32))
    combine = combine.astype(x.dtype)
    y = jnp.zeros_like(x)
    for e in range(MOE_EXPERTS):
        hdn = jax.nn.silu(x @ w_gate[e]) * (x @ w_up[e])
        y = y + combine[:, e:e + 1] * (hdn @ w_down[e])
    return y


def setup_inputs(seed: int = 0) -> dict:
    key = jax.random.key(seed)
    ks = iter(jax.random.split(key, 40))

    def nrm(shape, s):
        return jax.random.normal(next(ks), shape, jnp.float32) * s

    D = D_MODEL
    return {
        "x": nrm((BATCH, SEQ, D), 1.0),
        "c": nrm((BATCH, D), 1.0),
        "ctx": nrm((BATCH, CTX_LEN, D), 1.0),
        "c_ctx": nrm((D,), 1.0),
        "ada_w": nrm((DEPTH, D, 6 * D), 0.5 * D ** -0.5),
        "ada_b": nrm((DEPTH, 6 * D), 0.02),
        "even_w_in": nrm((N_EVEN, D, EVEN_IN), D ** -0.5),
        "even_dec_w_f": nrm((N_EVEN, GLA_RANK, GLA_HEADS * GLA_DK), GLA_RANK ** -0.5),
        "even_dec_b_f": nrm((N_EVEN, GLA_HEADS * GLA_DK), 0.1),
        "even_dec_w_b": nrm((N_EVEN, GLA_RANK, GLA_HEADS * GLA_DK), GLA_RANK ** -0.5),
        "even_dec_b_b": nrm((N_EVEN, GLA_HEADS * GLA_DK), 0.1),
        "even_gla_norm_g": 1.0 + nrm((N_EVEN, GLA_DV), 0.02),
        "even_conv_w": nrm((N_EVEN, CONV_WIDTH, 1, CONV_CH), CONV_WIDTH ** -0.5),
        "even_conv_b": nrm((N_EVEN, CONV_CH), 0.02),
        "even_conv_norm_g": 1.0 + nrm((N_EVEN, CONV_CH), 0.02),
        "even_conv_norm_b": nrm((N_EVEN, CONV_CH), 0.02),
        "even_w_out": nrm((N_EVEN, EVEN_MIX, D), EVEN_MIX ** -0.5),
        "odd_w_in": nrm((N_ODD, D, ODD_IN), D ** -0.5),
        "odd_q_norm_g": 1.0 + nrm((N_ODD, ATT_HEAD_DIM), 0.02),
        "odd_k_norm_g": 1.0 + nrm((N_ODD, ATT_HEAD_DIM), 0.02),
        "odd_w_out": nrm((N_ODD, ODD_MIX, D), ODD_MIX ** -0.5),
        "moe_w_group": nrm((DEPTH, D, MOE_GROUPS), D ** -0.5),
        "moe_b_group": nrm((DEPTH, MOE_GROUPS), 0.01),
        "moe_w_router": nrm((DEPTH, D, MOE_GROUPS, MOE_EXPERTS_PER_GROUP), D ** -0.5),
        "moe_b_router": nrm((DEPTH, MOE_GROUPS, MOE_EXPERTS_PER_GROUP), 0.01),
        "moe_w_gate": nrm((DEPTH, MOE_EXPERTS, D, MOE_HIDDEN), D ** -0.5),
        "moe_w_up": nrm((DEPTH, MOE_EXPERTS, D, MOE_HIDDEN), D ** -0.5),
        "moe_w_down": nrm((DEPTH, MOE_EXPERTS, MOE_HIDDEN, D), MOE_HIDDEN ** -0.5),
        "final_norm_g": 1.0 + nrm((D,), 0.02),
    }


def reference(x, c, ctx, c_ctx, ada_w, ada_b, even_w_in, even_dec_w_f, even_dec_b_f, even_dec_w_b,
              even_dec_b_b, even_gla_norm_g, even_conv_w, even_conv_b, even_conv_norm_g, even_conv_norm_b,
              even_w_out, odd_w_in, odd_q_norm_g, odd_k_norm_g, odd_w_out, moe_w_group, moe_b_group,
              moe_w_router, moe_b_router, moe_w_gate, moe_w_up, moe_w_down, final_norm_g):
    b_, t_len, d_ = x.shape
    c_len = ctx.shape[1]
    cos, sin = axial_rope_tables(t_len)
    silu_c = jax.nn.silu(c)
    silu_cc = jax.nn.silu(c_ctx)
    for l in range(DEPTH):
        last = l == DEPTH - 1
        i = l // 2
        mod = (silu_c @ ada_w[l] + ada_b[l])[:, None, :]
        sh1, sc1, g1, sh2, sc2, g2 = jnp.split(mod, 6, axis=-1)
        cmod = silu_cc @ ada_w[l] + ada_b[l]
        csh1, csc1, cg1, csh2, csc2, cg2 = jnp.split(cmod, 6, axis=-1)
        h_lat = modulate(x, sh1, sc1)
        h_ctx = modulate(ctx, csh1, csc1)
        if l % 2 == 0:
            params = (even_w_in[i], even_dec_w_f[i], even_dec_b_f[i], even_dec_w_b[i], even_dec_b_b[i],
                      even_gla_norm_g[i], even_conv_w[i], even_conv_b[i], even_conv_norm_g[i],
                      even_conv_norm_b[i], even_w_out[i])
            s_zero = jnp.zeros((b_, GLA_HEADS, GLA_DK, GLA_DV), jnp.float32)
            o_ctx, s_f, s_b = even_mixer(h_ctx, s_zero, s_zero, *params)
            o_lat, _, _ = even_mixer(h_lat, s_f, s_b, *params)
        else:
            o_lat, o_ctx = odd_mixer(h_lat, h_ctx, cos, sin, odd_w_in[i], odd_q_norm_g[i],
                                     odd_k_norm_g[i], odd_w_out[i], not last)
        x = x + g1 * o_lat
        moe_params = (moe_w_group[l], moe_b_group[l], moe_w_router[l], moe_b_router[l],
                      moe_w_gate[l], moe_w_up[l], moe_w_down[l])
        if last:
            y = hier_moe(modulate(x, sh2, sc2).reshape(-1, d_), *moe_params)
            x = x + g2 * y.reshape(b_, t_len, d_)
        else:
            ctx = ctx + cg1 * o_ctx
            tokens = jnp.concatenate([modulate(x, sh2, sc2).reshape(-1, d_),
                                      modulate(ctx, csh2, csc2).reshape(-1, d_)], axis=0)
            y = hier_moe(tokens, *moe_params)
            n_lat = b_ * t_len
            x = x + g2 * y[:n_lat].reshape(b_, t_len, d_)
            ctx = ctx + cg2 * y[n_lat:].reshape(b_, c_len, d_)
    return rms_norm(x) * final_norm_g
```

```python
import functools

import jax
import jax.numpy as jnp
import numpy as np
from jax import lax
from jax.experimental import pallas as pl
from jax.experimental.pallas import tpu as pltpu

F32 = jnp.float32
BF16 = jnp.bfloat16

D = 1024
CTX = 256
SEQ = 2048
S = CTX + SEQ
DEPTH = 4
GRID_W = 64
EPS = 1e-6
GLA_HEADS = 4
GLA_DV = 128
GLA_DK = 64
GLA_RANK = 16
GLA_NORMALIZER = 16.0
CONV_CH = 512
CONV_WIDTH = 31
ATT_HD = 128
ATT_HEADS = 8
ATT_KV = 2
ROPE_THETA = 10000.0
MOE_GROUPS = 4
MOE_EPG = 4
MOE_EXPERTS = 16
MOE_HIDDEN = 512
MOE_CLASSES = MOE_GROUPS * 6

EVEN_W = 2688
ODD_W = 1536
GLA_CHUNK = 128
CONV_RT = 64
ROUTE_W = D + 128
VMEM_LIMIT = 48 * 1024 * 1024


def _cparams(sem):
    return pltpu.CompilerParams(dimension_semantics=sem, vmem_limit_bytes=VMEM_LIMIT)


def _silu(v):
    return v * jax.nn.sigmoid(v)


def _mods_kernel(c_ref, w_ref, b_ref, o_ref):
    a = _silu(c_ref[...])
    o_ref[...] = jnp.dot(a, w_ref[...], preferred_element_type=F32,
                         precision=lax.Precision.HIGHEST) + b_ref[...]


def _mods(cvec, ada_w, ada_b):
    nl, _, n6 = ada_w.shape
    rows = cvec.shape[0]
    nb = n6 // D
    return pl.pallas_call(
        _mods_kernel,
        grid=(nl, nb),
        in_specs=[
            pl.BlockSpec((rows, D), lambda l, j: (0, 0)),
            pl.BlockSpec((None, D, D), lambda l, j: (l, 0, j)),
            pl.BlockSpec((None, 1, D), lambda l, j: (l, 0, j)),
        ],
        out_specs=pl.BlockSpec((None, rows, D), lambda l, j: (l, 0, j)),
        out_shape=jax.ShapeDtypeStruct((nl, rows, n6), F32),
        compiler_params=_cparams(("parallel", "parallel")),
        name="adaln_mods",
    )(cvec, ada_w, ada_b.reshape(nl, 1, n6))


def _row_select(mod_ref, idx, j, tm):
    srow = j * tm + lax.broadcasted_iota(jnp.int32, (tm, 1), 0)
    return jnp.where(srow < CTX, mod_ref[0, idx:idx + 1, :], mod_ref[1, idx:idx + 1, :])


def _modulated(x, mod_ref, j, tm, shift_idx, scale_idx):
    ms = jnp.mean(x * x, axis=-1, keepdims=True)
    xn = x * lax.rsqrt(ms + EPS)
    return xn * (1.0 + _row_select(mod_ref, scale_idx, j, tm)) + _row_select(mod_ref, shift_idx, j, tm)


def _inproj_even_kernel(x_ref, mod_ref, w_ref, o_ref, *, tm):
    j = pl.program_id(1)
    h = _modulated(x_ref[...], mod_ref, j, tm, 0, 1)
    o_ref[...] = jnp.dot(h.astype(BF16), w_ref[...], preferred_element_type=F32).astype(BF16)


def _inproj_odd_kernel(x_ref, mod_ref, w_ref, gain_ref, cos_ref, sin_ref, o_ref, *, tm):
    j = pl.program_id(1)
    h = _modulated(x_ref[...], mod_ref, j, tm, 0, 1)
    p = jnp.dot(h.astype(BF16), w_ref[...], preferred_element_type=F32)
    cos = cos_ref[...]
    sin = sin_ref[...]
    nqk = ATT_HEADS + ATT_KV
    for hd in range(nqk):
        sl = slice(hd * ATT_HD, (hd + 1) * ATT_HD)
        ph = p[:, sl]
        ms = jnp.mean(ph * ph, axis=-1, keepdims=True)
        ph = ph * lax.rsqrt(ms + EPS) * gain_ref[:, sl]
        ph = ph * cos + pltpu.roll(ph, ATT_HD // 2, axis=1) * sin
        o_ref[:, sl] = ph.astype(BF16)
    o_ref[:, nqk * ATT_HD:] = p[:, nqk * ATT_HD:].astype(BF16)


def _inproj(x, mod, w, width, tm, extra=None):
    n = x.shape[0]
    b = n // S
    tpb = S // tm
    in_specs = [
        pl.BlockSpec((tm, D), lambda i, j: (i * tpb + j, 0)),
        pl.BlockSpec((None, 2, 6, D), lambda i, j: (i, 0, 0, 0)),
        pl.BlockSpec((D, width), lambda i, j: (0, 0)),
    ]
    args = [x, mod, w]
    if extra is None:
        kern = functools.partial(_inproj_even_kernel, tm=tm)
        name = "inproj_even"
    else:
        gain, cos, sin = extra
        in_specs += [
            pl.BlockSpec((1, (ATT_HEADS + ATT_KV) * ATT_HD), lambda i, j: (0, 0)),
            pl.BlockSpec((tm, ATT_HD), lambda i, j: (j, 0)),
            pl.BlockSpec((tm, ATT_HD), lambda i, j: (j, 0)),
        ]
        args += [gain, cos, sin]
        kern = functools.partial(_inproj_odd_kernel, tm=tm)
        name = "inproj_odd"
    return pl.pallas_call(
        kern,
        grid=(b, tpb),
        in_specs=in_specs,
        out_specs=pl.BlockSpec((tm, width), lambda i, j: (i * tpb + j, 0)),
        out_shape=jax.ShapeDtypeStruct((n, width), BF16),
        compiler_params=_cparams(("parallel", "parallel")),
        name=name,
    )(*args)


def _log_sigmoid(v):
    return jnp.minimum(v, 0.0) - jnp.log(1.0 + jnp.exp(-jnp.abs(v)))


def _split3(v):
    hi = v.astype(BF16)
    r1 = v - hi.astype(F32)
    mid = r1.astype(BF16)
    lo = (r1 - mid.astype(F32)).astype(BF16)
    return hi, mid, lo


def _gla_kernel(qk_ref, v_ref, z_ref, r_ref, wdec_ref, bdec_ref, gn_ref, o_ref, of_ref, st_ref):
    c = GLA_CHUNK
    nch = S // c
    ncc = CTX // c
    hk = GLA_HEADS * GLA_DK
    rows = lax.broadcasted_iota(jnp.int32, (c, c), 0)
    cols = lax.broadcasted_iota(jnp.int32, (c, c), 1)
    tt = (((1,), (1,)), ((), ()))
    ta = (((0,), (0,)), ((), ()))

    def chunk(ci, fwd):
        r0 = pl.multiple_of(ci * c, c)
        qk = qk_ref[pl.ds(r0, c), :]
        v = v_ref[pl.ds(r0, c), :]
        z = z_ref[pl.ds(r0, c), :]
        g0 = 0 if fwd else hk
        gl = jnp.dot(z, wdec_ref[:, g0:g0 + hk], preferred_element_type=F32) + bdec_ref[:, g0:g0 + hk]
        la = _log_sigmoid(gl) * (1.0 / GLA_NORMALIZER)
        keep = (rows >= cols) if fwd else (rows <= cols)
        tri = jnp.where(keep, 1.0, 0.0).astype(BF16)
        hi, mid, lo = _split3(la)
        cum = (jnp.dot(tri, hi, preferred_element_type=F32)
               + jnp.dot(tri, mid, preferred_element_type=F32)
               + jnp.dot(tri, lo, preferred_element_type=F32))
        total = cum[c - 1:c, :] if fwd else cum[0:1, :]
        mref = cum[c // 2:c // 2 + 1, :]
        q = qk[:, 0:hk].astype(F32) * (GLA_DK ** -0.5)
        k = qk[:, hk:2 * hk].astype(F32)
        q_in = (q * jnp.exp(cum - mref)).astype(BF16)
        k_in = (k * jnp.exp(mref - cum)).astype(BF16)
        q_st = (q * jnp.exp(cum)).astype(BF16)
        k_end = (k * jnp.exp(total - cum)).astype(BF16)
        dec = jnp.exp(total)
        outs = []
        for h in range(GLA_HEADS):
            ks = slice(h * GLA_DK, (h + 1) * GLA_DK)
            vs = slice(h * GLA_DV, (h + 1) * GLA_DV)
            att = lax.dot_general(q_in[:, ks], k_in[:, ks], tt, preferred_element_type=F32)
            att = jnp.where(keep, att, 0.0).astype(BF16)
            st = st_ref[h]
            o = (jnp.dot(att, v[:, vs], preferred_element_type=F32)
                 + lax.dot_general(q_st[:, ks], st.astype(BF16), tt, preferred_element_type=F32))
            u = lax.dot_general(v[:, vs], k_end[:, ks], ta, preferred_element_type=F32)
            st_ref[h] = st * dec[:, ks] + u
            outs.append(o)
        return r0, outs

    st_ref[...] = jnp.zeros_like(st_ref)

    def fwd_body(t, carry):
        r0, outs = chunk(t, True)
        for h in range(GLA_HEADS):
            of_ref[pl.ds(r0, c), h * GLA_DV:(h + 1) * GLA_DV] = outs[h]
        return carry

    lax.fori_loop(0, nch, fwd_body, 0)

    st_ref[...] = jnp.zeros_like(st_ref)

    def bwd_body(t, carry):
        ci = jnp.where(t < ncc, ncc - 1 - t, nch - 1 - (t - ncc))
        r0, outs = chunk(ci, False)
        for h in range(GLA_HEADS):
            vs = slice(h * GLA_DV, (h + 1) * GLA_DV)
            o = of_ref[pl.ds(r0, c), vs] + outs[h]
            ms = jnp.mean(o * o, axis=-1, keepdims=True)
            o = o * lax.rsqrt(ms + EPS) * gn_ref[...]
            o_ref[pl.ds(r0, c), vs] = (o * _silu(r_ref[pl.ds(r0, c), vs].astype(F32))).astype(BF16)
        return carry

    lax.fori_loop(0, nch, bwd_body, 0)


def _gla(p, wdec, bdec, gn):
    n = p.shape[0]
    b = n // S
    hv = GLA_HEADS * GLA_DV
    return pl.pallas_call(
        _gla_kernel,
        grid=(b,),
        in_specs=[
            pl.BlockSpec((S, 512), lambda i: (i, 0)),
            pl.BlockSpec((S, 512), lambda i: (i, 1)),
            pl.BlockSpec((S, 128), lambda i: (i, 20)),
            pl.BlockSpec((S, 512), lambda i: (i, 2)),
            pl.BlockSpec((128, 512), lambda i: (0, 0)),
            pl.BlockSpec((1, 512), lambda i: (0, 0)),
            pl.BlockSpec((1, GLA_DV), lambda i: (0, 0)),
        ],
        out_specs=pl.BlockSpec((S, hv), lambda i: (i, 0)),
        out_shape=jax.ShapeDtypeStruct((n, hv), BF16),
        scratch_shapes=[
            pltpu.VMEM((S, hv), F32),
            pltpu.VMEM((GLA_HEADS, GLA_DV, GLA_DK), F32),
        ],
        compiler_params=_cparams(("parallel",)),
        name="gla",
    )(p, p, p, p, wdec, bdec, gn)


def _conv_kernel(ca_ref, cg_ref, w_ref, cb_ref, lg_ref, lb_ref, o_ref, u_ref):
    rt = CONV_RT
    pad = 16
    zero = jnp.zeros((pad, CONV_CH), F32)
    u_ref[0:pad, :] = zero
    u_ref[pad + CTX:2 * pad + CTX, :] = zero
    u_ref[2 * pad + S:3 * pad + S, :] = zero
    u_ref[pad:pad + CTX, :] = ca_ref[0:CTX, :].astype(F32) * jax.nn.sigmoid(cg_ref[0:CTX, :].astype(F32))
    u_ref[2 * pad + CTX:2 * pad + S, :] = (ca_ref[CTX:S, :].astype(F32)
                                         * jax.nn.sigmoid(cg_ref[CTX:S, :].astype(F32)))
    ntile_ctx = CTX // rt
    half = CONV_WIDTH // 2

    def body(ti, carry):
        base = pl.multiple_of(ti * rt + jnp.where(ti >= ntile_ctx, pad, 0), 8)
        win = u_ref[pl.ds(base, rt + 32), :]
        acc = jnp.zeros((rt, CONV_CH), F32)
        for b8 in range(8):
            sh = win[b8:b8 + rt + 24, :]
            for a8 in range(4):
                jt = 8 * a8 + b8 - (pad - half)
                if 0 <= jt < CONV_WIDTH:
                    acc = acc + sh[8 * a8:8 * a8 + rt, :] * w_ref[jt:jt + 1, :]
        acc = acc + cb_ref[...]
        mu = jnp.mean(acc, axis=-1, keepdims=True)
        dv = acc - mu
        var = jnp.mean(dv * dv, axis=-1, keepdims=True)
        y = dv * lax.rsqrt(var + EPS) * lg_ref[...] + lb_ref[...]
        o_ref[pl.ds(pl.multiple_of(ti * rt, rt), rt), :] = _silu(y).astype(BF16)
        return carry

    lax.fori_loop(0, S // rt, body, 0)


def _conv(p, w, cb, lg, lb):
    n = p.shape[0]
    b = n // S
    vec = pl.BlockSpec((1, CONV_CH), lambda i: (0, 0))
    return pl.pallas_call(
        _conv_kernel,
        grid=(b,),
        in_specs=[
            pl.BlockSpec((S, CONV_CH), lambda i: (i, 3)),
            pl.BlockSpec((S, CONV_CH), lambda i: (i, 4)),
            pl.BlockSpec((32, CONV_CH), lambda i: (0, 0)),
            vec, vec, vec,
        ],
        out_specs=pl.BlockSpec((S, CONV_CH), lambda i: (i, 0)),
        out_shape=jax.ShapeDtypeStruct((n, CONV_CH), BF16),
        scratch_shapes=[pltpu.VMEM((S + 48, CONV_CH), F32)],
        compiler_params=_cparams(("parallel",)),
        name="conv_module",
    )(p, p, w, cb, lg, lb)


def _attn_kernel(q_ref, k_ref, v_ref, o_ref):
    qi = pl.program_id(2)
    tt = (((1,), (1,)), ((), ()))
    group = ATT_HEADS // ATT_KV

    def run(nk):
        k = k_ref[0:nk, :]
        v = v_ref[0:nk, :]
        for g in range(group):
            sl = slice(g * ATT_HD, (g + 1) * ATT_HD)
            s = lax.dot_general(q_ref[:, sl], k, tt, preferred_element_type=F32)
            m = jnp.max(s, axis=-1, keepdims=True)
            e = jnp.exp(s - m)
            l = jnp.sum(e, axis=-1, keepdims=True)
            o = jnp.dot(e.astype(BF16), v, preferred_element_type=F32)
            o_ref[:, sl] = (o / l).astype(BF16)

    @pl.when(qi == 0)
    def _():
        run(CTX)

    @pl.when(qi > 0)
    def _():
        run(S)


def _attention(p, tq):
    n = p.shape[0]
    b = n // S
    tpb = S // tq
    group = ATT_HEADS // ATT_KV
    gw = group * ATT_HD
    return pl.pallas_call(
        _attn_kernel,
        grid=(b, ATT_KV, tpb),
        in_specs=[
            pl.BlockSpec((tq, gw), lambda i, kv, j: (i * tpb + j, kv)),
            pl.BlockSpec((S, ATT_HD), lambda i, kv, j: (i, ATT_HEADS + kv)),
            pl.BlockSpec((S, ATT_HD), lambda i, kv, j: (i, ATT_HEADS + ATT_KV + kv)),
        ],
        out_specs=pl.BlockSpec((tq, gw), lambda i, kv, j: (i * tpb + j, kv)),
        out_shape=jax.ShapeDtypeStruct((n, ATT_HEADS * ATT_HD), BF16),
        compiler_params=_cparams(("parallel", "parallel", "parallel")),
        name="gq_attention",
    )(p, p, p)


def _outproj_kernel(a_ref, b_ref, x_ref, mod_ref, w_ref, o_ref, *, tm):
    j = pl.program_id(1)
    half = D // 2
    acc = (jnp.dot(a_ref[...], w_ref[0:half, :], preferred_element_type=F32)
           + jnp.dot(b_ref[...], w_ref[half:D, :], preferred_element_type=F32))
    o_ref[...] = x_ref[...] + _row_select(mod_ref, 2, j, tm) * acc


def _outproj(a, a_col, bm, b_col, x, mod, w, tm):
    n = x.shape[0]
    b = n // S
    tpb = S // tm
    half = D // 2
    return pl.pallas_call(
        functools.partial(_outproj_kernel, tm=tm),
        grid=(b, tpb),
        in_specs=[
            pl.BlockSpec((tm, half), lambda i, j: (i * tpb + j, a_col)),
            pl.BlockSpec((tm, half), lambda i, j: (i * tpb + j, b_col)),
            pl.BlockSpec((tm, D), lambda i, j: (i * tpb + j, 0)),
            pl.BlockSpec((None, 2, 6, D), lambda i, j: (i, 0, 0, 0)),
            pl.BlockSpec((D, D), lambda i, j: (0, 0)),
        ],
        out_specs=pl.BlockSpec((tm, D), lambda i, j: (i * tpb + j, 0)),
        out_shape=jax.ShapeDtypeStruct((n, D), F32),
        compiler_params=_cparams(("parallel", "parallel")),
        name="outproj_residual",
    )(a, bm, x, mod, w)


def _router_kernel(x_ref, mod_ref, w_ref, b_ref, hr_ref, info_ref, cnt_ref, run_ref, *, tm):
    i = pl.program_id(0)
    j = pl.program_id(1)

    @pl.when((i == 0) & (j == 0))
    def _():
        run_ref[...] = jnp.zeros_like(run_ref)

    h = _modulated(x_ref[...], mod_ref, j, tm, 3, 4)
    h_hi = h.astype(BF16)
    h_lo = (h - h_hi.astype(F32)).astype(BF16)
    w = w_ref[...]
    pa = jnp.dot(h_hi, w, preferred_element_type=F32)
    pb = jnp.dot(h_lo, w, preferred_element_type=F32)
    logits = pa + pltpu.roll(pa, 128 - 32, axis=1) + pb + b_ref[...]
    lane = lax.broadcasted_iota(jnp.int32, (tm, 128), 1).astype(F32)
    neg = jnp.float32(-jnp.inf)
    big = jnp.float32(1024.0)

    gl = jnp.where(lane < MOE_GROUPS, logits, neg)
    gmax = jnp.max(gl, axis=-1, keepdims=True)
    gidx = jnp.min(jnp.where(gl == gmax, lane, big), axis=-1, keepdims=True)
    gsum = jnp.sum(jnp.exp(gl - gmax), axis=-1, keepdims=True)
    p_group = 1.0 / gsum

    base = MOE_GROUPS + MOE_EPG * gidx
    el = jnp.where((lane >= base) & (lane < base + MOE_EPG), logits, neg)
    emax = jnp.max(el, axis=-1, keepdims=True)
    i1 = jnp.min(jnp.where(el == emax, lane, big), axis=-1, keepdims=True)
    esum = jnp.sum(jnp.exp(el - emax), axis=-1, keepdims=True)
    el2 = jnp.where(lane == i1, neg, el)
    e2max = jnp.max(el2, axis=-1, keepdims=True)
    i2 = jnp.min(jnp.where(el2 == e2max, lane, big), axis=-1, keepdims=True)
    p1 = 1.0 / esum
    p2 = jnp.exp(e2max - emax) / esum
    w1 = p_group * p1 / (p1 + p2)
    w2 = p_group * p2 / (p1 + p2)

    a1 = i1 - base
    a2 = i2 - base
    first_low = a1 < a2
    lo = jnp.where(first_low, a1, a2)
    hi = jnp.where(first_low, a2, a1)
    w_lo = jnp.where(first_low, w1, w2)
    w_hi = jnp.where(first_low, w2, w1)
    pair = jnp.where(lo == 0.0, hi - 1.0, jnp.where(lo == 1.0, hi + 1.0, 5.0))
    cls = gidx * 6.0 + pair

    onehot = jnp.where(lane == cls, 1.0, 0.0)
    r_i = lax.broadcasted_iota(jnp.int32, (tm, tm), 0)
    c_i = lax.broadcasted_iota(jnp.int32, (tm, tm), 1)
    strict = jnp.where(r_i > c_i, 1.0, 0.0).astype(BF16)
    before = jnp.dot(strict, onehot.astype(BF16), preferred_element_type=F32)
    rank = jnp.sum(onehot * (before + run_ref[...]), axis=-1, keepdims=True)
    run_ref[...] = run_ref[...] + jnp.sum(onehot, axis=0, keepdims=True)

    route = jnp.where(lane == 0, cls,
                      jnp.where(lane == 1, rank,
                                jnp.where(lane == 2, w_lo,
                                          jnp.where(lane == 3, w_hi, 0.0))))
    hr_ref[:, 0:D] = h
    hr_ref[:, D:ROUTE_W] = route
    info_ref[...] = route
    cnt_ref[...] = run_ref[...]


def _router(x, mod, w2, bias, tm):
    n = x.shape[0]
    b = n // S
    tpb = S // tm
    return pl.pallas_call(
        functools.partial(_router_kernel, tm=tm),
        grid=(b, tpb),
        in_specs=[
            pl.BlockSpec((tm, D), lambda i, j: (i * tpb + j, 0)),
            pl.BlockSpec((None, 2, 6, D), lambda i, j: (i, 0, 0, 0)),
            pl.BlockSpec((D, 128), lambda i, j: (0, 0)),
            pl.BlockSpec((1, 128), lambda i, j: (0, 0)),
        ],
        out_specs=[
            pl.BlockSpec((tm, ROUTE_W), lambda i, j: (i * tpb + j, 0)),
            pl.BlockSpec((tm, 128), lambda i, j: (i * tpb + j, 0)),
            pl.BlockSpec((1, 128), lambda i, j: (0, 0)),
        ],
        out_shape=[
            jax.ShapeDtypeStruct((n, ROUTE_W), F32),
            jax.ShapeDtypeStruct((n, 128), F32),
            jax.ShapeDtypeStruct((1, 128), F32),
        ],
        scratch_shapes=[pltpu.VMEM((1, 128), F32)],
        compiler_params=_cparams(("arbitrary", "arbitrary")),
        name="moe_router",
    )(x, mod, w2, bias)


def _dispatch_kernel(pos_ref, hr_ref, xs_in_ref, xs_ref, sem, *, td):
    del xs_in_ref
    base = pl.program_id(0) * td

    def body(r, carry):
        dst = pos_ref[0, 0, r]
        pltpu.make_async_copy(hr_ref.at[pl.ds(base + r, 1)], xs_ref.at[pl.ds(dst, 1)], sem).start()
        return carry

    lax.fori_loop(0, td, body, 0)
    pltpu.make_async_copy(hr_ref.at[pl.ds(0, td)], xs_ref.at[pl.ds(0, td)], sem).wait()


def _dispatch(pos, hr, xs_init, td):
    n = hr.shape[0]
    return pl.pallas_call(
        functools.partial(_dispatch_kernel, td=td),
        grid=(n // td,),
        in_specs=[
            pl.BlockSpec((1, 1, td), lambda i: (i, 0, 0), memory_space=pltpu.SMEM),
            pl.BlockSpec(memory_space=pl.ANY),
            pl.BlockSpec(memory_space=pl.ANY),
        ],
        out_specs=pl.BlockSpec(memory_space=pl.ANY),
        out_shape=jax.ShapeDtypeStruct(xs_init.shape, F32),
        scratch_shapes=[pltpu.SemaphoreType.DMA],
        input_output_aliases={2: 0},
        compiler_params=_cparams(("arbitrary",)),
        name="moe_dispatch",
    )(pos.reshape(n // td, 1, td), hr, xs_init)


def _expert_kernel(be_ref, xs_ref, wga_ref, wua_ref, wda_ref, wgb_ref, wub_ref, wdb_ref, ys_ref):
    del be_ref
    x = xs_ref[:, 0:D].astype(BF16)
    w_lo = xs_ref[:, D + 2:D + 3]
    w_hi = xs_ref[:, D + 3:D + 4]

    def expert(wg, wu, wd):
        g = jnp.dot(x, wg[...], preferred_element_type=F32)
        u = jnp.dot(x, wu[...], preferred_element_type=F32)
        hdn = (_silu(g) * u).astype(BF16)
        return jnp.dot(hdn, wd[...], preferred_element_type=F32)

    ys_ref[...] = w_lo * expert(wga_ref, wua_ref, wda_ref) + w_hi * expert(wgb_ref, wub_ref, wdb_ref)


def _experts(blk_e, xs, wg, wu, wd, tmb):
    npad = xs.shape[0]
    nblk = npad // tmb

    def wspec(shape, which):
        return pl.BlockSpec((None,) + shape, lambda i, be: (be[which, i], 0, 0))

    grid_spec = pltpu.PrefetchScalarGridSpec(
        num_scalar_prefetch=1,
        grid=(nblk,),
        in_specs=[
            pl.BlockSpec((tmb, ROUTE_W), lambda i, be: (i, 0)),
            wspec((D, MOE_HIDDEN), 0), wspec((D, MOE_HIDDEN), 0), wspec((MOE_HIDDEN, D), 0),
            wspec((D, MOE_HIDDEN), 1), wspec((D, MOE_HIDDEN), 1), wspec((MOE_HIDDEN, D), 1),
        ],
        out_specs=pl.BlockSpec((tmb, D), lambda i, be: (i, 0)),
    )
    return pl.pallas_call(
        _expert_kernel,
        grid_spec=grid_spec,
        out_shape=jax.ShapeDtypeStruct((npad, D), F32),
        compiler_params=_cparams(("arbitrary",)),
        name="moe_experts",
    )(blk_e, xs, wg, wu, wd, wg, wu, wd)


def _combine_kernel(pos_ref, ys_ref, x_ref, mod_ref, fg_ref, o_ref, buf, sem, *, tc, final):
    j = pl.program_id(1)

    def body(r, carry):
        src = pos_ref[0, 0, r]
        pltpu.make_async_copy(ys_ref.at[pl.ds(src, 1)], buf.at[pl.ds(r, 1)], sem).start()
        return carry

    lax.fori_loop(0, tc, body, 0)
    pltpu.make_async_copy(ys_ref.at[pl.ds(0, tc)], buf, sem).wait()
    o = x_ref[...] + _row_select(mod_ref, 5, j, tc) * buf[...]
    if final:
        ms = jnp.mean(o * o, axis=-1, keepdims=True)
        o = o * lax.rsqrt(ms + EPS) * fg_ref[...]
    o_ref[...] = o


def _combine(pos, ys, x, mod, fg, tc, final):
    n = x.shape[0]
    b = n // S
    tpb = S // tc
    return pl.pallas_call(
        functools.partial(_combine_kernel, tc=tc, final=final),
        grid=(b, tpb),
        in_specs=[
            pl.BlockSpec((1, 1, tc), lambda i, j: (i * tpb + j, 0, 0), memory_space=pltpu.SMEM),
            pl.BlockSpec(memory_space=pl.ANY),
            pl.BlockSpec((tc, D), lambda i, j: (i * tpb + j, 0)),
            pl.BlockSpec((None, 2, 6, D), lambda i, j: (i, 0, 0, 0)),
            pl.BlockSpec((1, D), lambda i, j: (0, 0)),
        ],
        out_specs=pl.BlockSpec((tc, D), lambda i, j: (i * tpb + j, 0)),
        out_shape=jax.ShapeDtypeStruct((n, D), F32),
        scratch_shapes=[pltpu.VMEM((tc, D), F32), pltpu.SemaphoreType.DMA],
        compiler_params=_cparams(("arbitrary", "arbitrary")),
        name="moe_combine",
    )(pos.reshape(n // tc, 1, tc), ys, x, mod, fg)


_PAIR_LO = np.array([0, 0, 0, 1, 1, 2], np.int32)
_PAIR_HI = np.array([1, 2, 3, 2, 3, 3], np.int32)


def _even_weights(w_in, dec_w_f, dec_b_f, dec_w_b, dec_b_b):
    hk = GLA_HEADS * GLA_DK
    hv = GLA_HEADS * GLA_DV
    o_q, o_k, o_v, o_r = 0, hk, 2 * hk, 2 * hk + hv
    o_zf = o_r + hv
    o_zb = o_zf + GLA_RANK
    o_ca = o_zb + GLA_RANK
    o_cg = o_ca + CONV_CH
    main = jnp.concatenate([w_in[:, o_q:o_zf], w_in[:, o_ca:o_cg + CONV_CH]], axis=1)
    gates = jnp.concatenate([w_in[:, o_zf:o_ca], jnp.zeros((D, 128 - 2 * GLA_RANK), F32)], axis=1)
    w = jnp.concatenate([main, gates], axis=1).astype(BF16)
    wdec = jnp.zeros((128, 2 * hk), F32)
    wdec = wdec.at[0:GLA_RANK, 0:hk].set(dec_w_f).at[GLA_RANK:2 * GLA_RANK, hk:].set(dec_w_b)
    bdec = jnp.concatenate([dec_b_f, dec_b_b]).reshape(1, 2 * hk)
    return w, wdec.astype(BF16), bdec


def _odd_weights(w_in, q_norm_g, k_norm_g):
    perm = np.concatenate([np.arange(0, ATT_HD, 2), np.arange(1, ATT_HD, 2)])
    nqk = ATT_HEADS + ATT_KV
    cols = np.concatenate([h * ATT_HD + perm for h in range(nqk)]
                          + [np.arange(nqk * ATT_HD, ODD_W)])
    w = w_in[:, cols].astype(BF16)
    gq = q_norm_g[perm] * (ATT_HD ** -0.5)
    gain = jnp.concatenate([jnp.tile(gq, ATT_HEADS), jnp.tile(k_norm_g[perm], ATT_KV)]).reshape(1, -1)
    return w, gain


def _rope_tables():
    t = jnp.arange(SEQ)
    row = (t // GRID_W).astype(F32)
    col = (t % GRID_W).astype(F32)
    ppa = ATT_HD // 4
    inv = ROPE_THETA ** (-jnp.arange(ppa, dtype=F32) / ppa)
    ang = jnp.concatenate([row[:, None] * inv, col[:, None] * inv], axis=-1)
    cos = jnp.cos(ang)
    sin = jnp.sin(ang)
    cos_t = jnp.concatenate([jnp.ones((CTX, ATT_HD), F32), jnp.concatenate([cos, cos], axis=1)], axis=0)
    sin_t = jnp.concatenate([jnp.zeros((CTX, ATT_HD), F32), jnp.concatenate([-sin, sin], axis=1)], axis=0)
    return cos_t, sin_t


def _router_weights(w_group, b_group, w_router, b_router):
    wl = jnp.concatenate([w_group, w_router.reshape(D, MOE_EXPERTS)], axis=1)
    hi = wl.astype(BF16)
    lo = (wl - hi.astype(F32)).astype(BF16)
    nl = MOE_GROUPS + MOE_EXPERTS
    w2 = jnp.zeros((D, 128), BF16).at[:, 0:nl].set(hi).at[:, 32:32 + nl].set(lo)
    bias = jnp.zeros((1, 128), F32).at[0, 0:nl].set(jnp.concatenate([b_group, b_router.reshape(-1)]))
    return w2, bias


def _moe(x, mod, w2, bias, wg, wu, wd, fg, final, tm, tmb, td, tc):
    n = x.shape[0]
    hr, info, cnt = _router(x, mod, w2, bias, tm)
    counts = cnt[0, 0:MOE_CLASSES].astype(jnp.int32)
    padded = ((counts + tmb - 1) // tmb) * tmb
    ends = jnp.cumsum(padded)
    offs = ends - padded
    cls = info[:, 0].astype(jnp.int32)
    pos = offs[cls] + info[:, 1].astype(jnp.int32)
    npad = n + MOE_CLASSES * tmb
    nblk = npad // tmb
    blk_cls = jnp.minimum(jnp.searchsorted(ends, jnp.arange(nblk, dtype=jnp.int32) * tmb, side="right"),
                          MOE_CLASSES - 1).astype(jnp.int32)
    grp = blk_cls // 6
    pr = blk_cls % 6
    blk_e = jnp.stack([grp * MOE_EPG + jnp.asarray(_PAIR_LO)[pr],
                       grp * MOE_EPG + jnp.asarray(_PAIR_HI)[pr]], axis=0).astype(jnp.int32)
    xs = _dispatch(pos, hr, jnp.zeros((npad, ROUTE_W), F32), td)
    ys = _experts(blk_e, xs, wg, wu, wd, tmb)
    return _combine(pos, ys, x, mod, fg, tc, final)


def kernel(x, c, ctx, c_ctx, ada_w, ada_b, even_w_in, even_dec_w_f, even_dec_b_f, even_dec_w_b,
           even_dec_b_b, even_gla_norm_g, even_conv_w, even_conv_b, even_conv_norm_g, even_conv_norm_b,
           even_w_out, odd_w_in, odd_q_norm_g, odd_k_norm_g, odd_w_out, moe_w_group, moe_b_group,
           moe_w_router, moe_b_router, moe_w_gate, moe_w_up, moe_w_down, final_norm_g):
    b = x.shape[0]
    n = b * S
    tm = 256

    cvec = jnp.zeros((40, D), F32).at[0:b].set(c).at[b].set(c_ctx)
    mods = _mods(cvec, ada_w, ada_b)
    mod_lat = mods[:, 0:b].reshape(DEPTH, b, 1, 6, D)
    mod_ctx = jnp.broadcast_to(mods[:, b].reshape(DEPTH, 1, 1, 6, D), (DEPTH, b, 1, 6, D))
    mod_all = jnp.concatenate([mod_ctx, mod_lat], axis=2)

    xc = jnp.concatenate([ctx, x], axis=1).reshape(n, D)
    cos_t, sin_t = _rope_tables()
    fg = final_norm_g.reshape(1, D)

    for l in range(DEPTH):
        i = l // 2
        mod = mod_all[l]
        if l % 2 == 0:
            w, wdec, bdec = _even_weights(even_w_in[i], even_dec_w_f[i], even_dec_b_f[i],
                                          even_dec_w_b[i], even_dec_b_b[i])
            p = _inproj(xc, mod, w, EVEN_W, tm)
            og = _gla(p, wdec, bdec, even_gla_norm_g[i].reshape(1, GLA_DV))
            cw = jnp.concatenate([even_conv_w[i].reshape(CONV_WIDTH, CONV_CH),
                                  jnp.zeros((1, CONV_CH), F32)], axis=0)
            oc = _conv(p, cw, even_conv_b[i].reshape(1, -1), even_conv_norm_g[i].reshape(1, -1),
                       even_conv_norm_b[i].reshape(1, -1))
            xc = _outproj(og, 0, oc, 0, xc, mod, even_w_out[i].astype(BF16), tm)
        else:
            w, gain = _odd_weights(odd_w_in[i], odd_q_norm_g[i], odd_k_norm_g[i])
            p = _inproj(xc, mod, w, ODD_W, tm, extra=(gain, cos_t, sin_t))
            oa = _attention(p, 256)
            xc = _outproj(oa, 0, oa, 1, xc, mod, odd_w_out[i].astype(BF16), tm)
        w2, bias = _router_weights(moe_w_group[l], moe_b_group[l], moe_w_router[l], moe_b_router[l])
        xc = _moe(xc, mod, w2, bias, moe_w_gate[l].astype(BF16), moe_w_up[l].astype(BF16),
                  moe_w_down[l].astype(BF16), fg, l == DEPTH - 1, tm, 256, 2304, 768)

    return xc.reshape(b, S, D)[:, CTX:, :]
```

```python
import functools

import jax
import jax.numpy as jnp
import numpy as np
from jax import lax
from jax.experimental import pallas as pl
from jax.experimental.pallas import tpu as pltpu

F32 = jnp.float32
BF16 = jnp.bfloat16

D = 1024
CTX = 256
SEQ = 2048
S = CTX + SEQ
DEPTH = 4
GRID_W = 64
EPS = 1e-6
GLA_HEADS = 4
GLA_DV = 128
GLA_DK = 64
GLA_RANK = 16
GLA_NORMALIZER = 16.0
CONV_CH = 512
CONV_WIDTH = 31
ATT_HD = 128
ATT_HEADS = 8
ATT_KV = 2
ROPE_THETA = 10000.0
MOE_GROUPS = 4
MOE_EPG = 4
MOE_EXPERTS = 16
MOE_HIDDEN = 512
MOE_CLASSES = MOE_GROUPS * 6

EVEN_W = 2688
ODD_W = 1536
GLA_CHUNK = 128
CONV_RT = 64
CONV_STRAND_CTX = 36
CONV_STRAND_LAT = 260
CONV_U_CTX = 16
CONV_U_LAT = CONV_U_CTX + 8 * CONV_STRAND_CTX + 32
CONV_U_ROWS = CONV_U_LAT + 8 * CONV_STRAND_LAT + 16
CONV_V_LAT = 8 * CONV_STRAND_CTX
CONV_V_ROWS = CONV_V_LAT + 8 * CONV_STRAND_LAT
ROUTE_W = D + 128
DMA_UNROLL = 8
MOE_BLOCK = 256
ROUTER_TILE = 256
PROJ_TILE = 768
QK_PROJ_TILE = 256
ROW_DMA_TILE = 768
VMEM_LIMIT = 48 * 1024 * 1024


def _cparams(sem):
    return pltpu.CompilerParams(dimension_semantics=sem, vmem_limit_bytes=VMEM_LIMIT)


def _sigmoid(v):
    return 0.5 * jnp.tanh(0.5 * v) + 0.5


def _silu(v):
    return v * _sigmoid(v)


def _mods_kernel(c_ref, w_ref, b_ref, o_ref):
    a = _silu(c_ref[...])
    o_ref[...] = jnp.dot(a, w_ref[...], preferred_element_type=F32,
                         precision=lax.Precision.HIGHEST) + b_ref[...]


def _mods(cvec, ada_w, ada_b):
    nl, _, n6 = ada_w.shape
    rows = cvec.shape[0]
    nb = n6 // D
    return pl.pallas_call(
        _mods_kernel,
        grid=(nl, nb),
        in_specs=[
            pl.BlockSpec((rows, D), lambda l, j: (0, 0)),
            pl.BlockSpec((None, D, D), lambda l, j: (l, 0, j)),
            pl.BlockSpec((None, 1, D), lambda l, j: (l, 0, j)),
        ],
        out_specs=pl.BlockSpec((None, rows, D), lambda l, j: (l, 0, j)),
        out_shape=jax.ShapeDtypeStruct((nl, rows, n6), F32),
        compiler_params=_cparams(("parallel", "parallel")),
        name="adaln_mods",
    )(cvec, ada_w, ada_b.reshape(nl, 1, n6))


def _row_select(mod_ref, idx, j, tm):
    srow = j * tm + lax.broadcasted_iota(jnp.int32, (tm, 1), 0)
    return jnp.where(srow < CTX, mod_ref[0, idx:idx + 1, :], mod_ref[1, idx:idx + 1, :])


def _modulated(x, mod_ref, j, tm, shift_idx, scale_idx):
    ms = jnp.mean(x * x, axis=-1, keepdims=True)
    xn = x * lax.rsqrt(ms + EPS)
    return xn * (1.0 + _row_select(mod_ref, scale_idx, j, tm)) + _row_select(mod_ref, shift_idx, j, tm)


def _inproj_even_kernel(x_ref, mod_ref, w_ref, o_ref, *, tm):
    j = pl.program_id(1)
    h = _modulated(x_ref[...], mod_ref, j, tm, 0, 1)
    o_ref[...] = jnp.dot(h.astype(BF16), w_ref[...], preferred_element_type=F32).astype(BF16)


def _inproj_odd_kernel(x_ref, mod_ref, w_ref, gain_ref, cos_ref, sin_ref, o_ref, *, tm):
    j = pl.program_id(1)
    h = _modulated(x_ref[...], mod_ref, j, tm, 0, 1)
    p = jnp.dot(h.astype(BF16), w_ref[...], preferred_element_type=F32)
    cos = cos_ref[...]
    sin = sin_ref[...]
    nqk = ATT_HEADS + ATT_KV
    for hd in range(nqk):
        sl = slice(hd * ATT_HD, (hd + 1) * ATT_HD)
        ph = p[:, sl]
        ms = jnp.mean(ph * ph, axis=-1, keepdims=True)
        ph = ph * lax.rsqrt(ms + EPS) * gain_ref[:, sl]
        ph = ph * cos + pltpu.roll(ph, ATT_HD // 2, axis=1) * sin
        o_ref[:, sl] = ph.astype(BF16)
    o_ref[:, nqk * ATT_HD:] = p[:, nqk * ATT_HD:].astype(BF16)


def _inproj(x, mod, w, width, tm, extra=None):
    n = x.shape[0]
    b = n // S
    tpb = S // tm
    in_specs = [
        pl.BlockSpec((tm, D), lambda i, j: (i * tpb + j, 0)),
        pl.BlockSpec((None, 2, 6, D), lambda i, j: (i, 0, 0, 0)),
        pl.BlockSpec((D, width), lambda i, j: (0, 0)),
    ]
    args = [x, mod, w]
    if extra is None:
        kern = functools.partial(_inproj_even_kernel, tm=tm)
        name = "inproj_even"
    else:
        gain, cos, sin = extra
        in_specs += [
            pl.BlockSpec((1, (ATT_HEADS + ATT_KV) * ATT_HD), lambda i, j: (0, 0)),
            pl.BlockSpec((tm, ATT_HD), lambda i, j: (j, 0)),
            pl.BlockSpec((tm, ATT_HD), lambda i, j: (j, 0)),
        ]
        args += [gain, cos, sin]
        kern = functools.partial(_inproj_odd_kernel, tm=tm)
        name = "inproj_odd"
    return pl.pallas_call(
        kern,
        grid=(b, tpb),
        in_specs=in_specs,
        out_specs=pl.BlockSpec((tm, width), lambda i, j: (i * tpb + j, 0)),
        out_shape=jax.ShapeDtypeStruct((n, width), BF16),
        compiler_params=_cparams(("parallel", "parallel")),
        name=name,
    )(*args)


def _log_sigmoid(v):
    return jnp.minimum(v, 0.0) - jnp.log(1.0 + jnp.exp(-jnp.abs(v)))


def _split3(v):
    hi = v.astype(BF16)
    r1 = v - hi.astype(F32)
    mid = r1.astype(BF16)
    lo = (r1 - mid.astype(F32)).astype(BF16)
    return hi, mid, lo


def _gla_kernel(qk_ref, v_ref, z_ref, r_ref, wdec_ref, bdec_ref, gn_ref, o_ref, of_ref, ob_ref, st_ref):
    c = GLA_CHUNK
    nch = S // c
    ncc = CTX // c
    hk = GLA_HEADS * GLA_DK
    rows = lax.broadcasted_iota(jnp.int32, (c, c), 0)
    cols = lax.broadcasted_iota(jnp.int32, (c, c), 1)
    tt = (((1,), (1,)), ((), ()))
    ta = (((0,), (0,)), ((), ()))

    def chunk(ci, fwd):
        r0 = pl.multiple_of(ci * c, c)
        qk = qk_ref[pl.ds(r0, c), :]
        v = v_ref[pl.ds(r0, c), :]
        z = z_ref[pl.ds(r0, c), :]
        g0 = 0 if fwd else hk
        gl = jnp.dot(z, wdec_ref[:, g0:g0 + hk], preferred_element_type=F32) + bdec_ref[:, g0:g0 + hk]
        la = _log_sigmoid(gl) * (1.0 / GLA_NORMALIZER)
        keep = (rows >= cols) if fwd else (rows <= cols)
        tri = jnp.where(keep, 1.0, 0.0).astype(BF16)
        hi, mid, lo = _split3(la)
        cum = (jnp.dot(tri, hi, preferred_element_type=F32)
               + jnp.dot(tri, mid, preferred_element_type=F32)
               + jnp.dot(tri, lo, preferred_element_type=F32))
        total = cum[c - 1:c, :] if fwd else cum[0:1, :]
        mref = cum[c // 2:c // 2 + 1, :]
        q = qk[:, 0:hk].astype(F32) * (GLA_DK ** -0.5)
        k = qk[:, hk:2 * hk].astype(F32)
        q_in = (q * jnp.exp(cum - mref)).astype(BF16)
        k_in = (k * jnp.exp(mref - cum)).astype(BF16)
        q_st = (q * jnp.exp(cum)).astype(BF16)
        k_end = (k * jnp.exp(total - cum)).astype(BF16)
        dec = jnp.exp(total)
        for h in range(GLA_HEADS):
            ks = slice(h * GLA_DK, (h + 1) * GLA_DK)
            vs = slice(h * GLA_DV, (h + 1) * GLA_DV)
            att = lax.dot_general(q_in[:, ks], k_in[:, ks], tt, preferred_element_type=F32)
            att = jnp.where(keep, att, 0.0).astype(BF16)
            sref = st_ref.at[0 if fwd else 1]
            st = sref[h]
            o = (jnp.dot(att, v[:, vs], preferred_element_type=F32)
                 + lax.dot_general(q_st[:, ks], st.astype(BF16), tt, preferred_element_type=F32))
            u = lax.dot_general(v[:, vs], k_end[:, ks], ta, preferred_element_type=F32)
            sref[h] = st * dec[:, ks] + u
            dst = of_ref if fwd else ob_ref
            dst[pl.ds(r0, c), vs] = o

    st_ref[...] = jnp.zeros_like(st_ref)

    def scan_body(t, carry):
        chunk(t, True)
        chunk(jnp.where(t < ncc, ncc - 1 - t, nch - 1 - (t - ncc)), False)
        return carry

    lax.fori_loop(0, nch, scan_body, 0)

    def out_body(t, carry):
        r0 = pl.multiple_of(t * c, c)
        for h in range(GLA_HEADS):
            vs = slice(h * GLA_DV, (h + 1) * GLA_DV)
            o = of_ref[pl.ds(r0, c), vs] + ob_ref[pl.ds(r0, c), vs]
            ms = jnp.mean(o * o, axis=-1, keepdims=True)
            o = o * lax.rsqrt(ms + EPS) * gn_ref[...]
            o_ref[pl.ds(r0, c), vs] = (o * _silu(r_ref[pl.ds(r0, c), vs].astype(F32))).astype(BF16)
        return carry

    lax.fori_loop(0, nch, out_body, 0)


def _gla(p, wdec, bdec, gn):
    n = p.shape[0]
    b = n // S
    hv = GLA_HEADS * GLA_DV
    return pl.pallas_call(
        _gla_kernel,
        grid=(b,),
        in_specs=[
            pl.BlockSpec((S, 512), lambda i: (i, 0)),
            pl.BlockSpec((S, 512), lambda i: (i, 1)),
            pl.BlockSpec((S, 128), lambda i: (i, 20)),
            pl.BlockSpec((S, 512), lambda i: (i, 2)),
            pl.BlockSpec((128, 512), lambda i: (0, 0)),
            pl.BlockSpec((1, 512), lambda i: (0, 0)),
            pl.BlockSpec((1, GLA_DV), lambda i: (0, 0)),
        ],
        out_specs=pl.BlockSpec((S, hv), lambda i: (i, 0)),
        out_shape=jax.ShapeDtypeStruct((n, hv), BF16),
        scratch_shapes=[
            pltpu.VMEM((S, hv), F32),
            pltpu.VMEM((S, hv), F32),
            pltpu.VMEM((2, GLA_HEADS, GLA_DV, GLA_DK), F32),
        ],
        compiler_params=_cparams(("parallel",)),
        name="gla",
    )(p, p, p, p, wdec, bdec, gn)


def _conv_kernel(ca_ref, cg_ref, w_ref, cb_ref, lg_ref, lb_ref, o_ref, u_ref, v_ref):
    half = CONV_WIDTH // 2
    nslab = CONV_CH // 128
    for sl in range(nslab):
        ls = slice(sl * 128, (sl + 1) * 128)
        u_ref[sl, 0:CONV_U_CTX, :] = jnp.zeros((CONV_U_CTX, 128), F32)
        u_ref[sl, CONV_U_CTX + CTX:CONV_U_LAT, :] = jnp.zeros((CONV_U_LAT - CONV_U_CTX - CTX, 128), F32)
        u_ref[sl, CONV_U_LAT + SEQ:CONV_U_ROWS, :] = jnp.zeros((CONV_U_ROWS - CONV_U_LAT - SEQ, 128), F32)
        u_ref[sl, CONV_U_CTX:CONV_U_CTX + CTX, :] = (
            ca_ref[0:CTX, ls].astype(F32) * _sigmoid(cg_ref[0:CTX, ls].astype(F32)))
        u_ref[sl, CONV_U_LAT:CONV_U_LAT + SEQ, :] = (
            ca_ref[CTX:S, ls].astype(F32) * _sigmoid(cg_ref[CTX:S, ls].astype(F32)))

    def segment(ubase, vbase, strand, rb):
        for sl in range(nslab):
            ls = slice(sl * 128, (sl + 1) * 128)

            def body(blk, carry):
                l0 = blk * rb
                row0 = ubase + l0 - half
                win = [u_ref[sl, pl.ds(row0 + i, 8, stride=strand), :] for i in range(rb + 2 * half)]
                accs = [None] * rb
                for j in range(CONV_WIDTH):
                    wj = w_ref[j, :, ls]
                    for i in range(rb):
                        term = win[i + j] * wj
                        accs[i] = term if j == 0 else accs[i] + term
                for i in range(rb):
                    v_ref[sl, pl.ds(vbase + l0 + i, 8, stride=strand), :] = accs[i]
                return carry

            lax.fori_loop(0, strand // rb, body, 0)

    segment(CONV_U_CTX, 0, CONV_STRAND_CTX, 12)
    segment(CONV_U_LAT, CONV_V_LAT, CONV_STRAND_LAT, 10)

    rt = CONV_RT
    ntile_ctx = CTX // rt

    def norm_body(t, carry):
        r0 = pl.multiple_of(t * rt, rt)
        vrow = pl.multiple_of(r0 + jnp.where(t >= ntile_ctx, CONV_V_LAT - CTX, 0), 8)
        ys = [v_ref[sl, pl.ds(vrow, rt), :] + cb_ref[:, sl * 128:(sl + 1) * 128] for sl in range(nslab)]
        mu = jnp.sum(sum(ys), axis=-1, keepdims=True) * (1.0 / CONV_CH)
        dvs = [y - mu for y in ys]
        var = jnp.sum(sum(d * d for d in dvs), axis=-1, keepdims=True) * (1.0 / CONV_CH)
        inv = lax.rsqrt(var + EPS)
        for sl in range(nslab):
            ls = slice(sl * 128, (sl + 1) * 128)
            y = dvs[sl] * inv * lg_ref[:, ls] + lb_ref[:, ls]
            o_ref[pl.ds(r0, rt), ls] = _silu(y).astype(BF16)
        return carry

    lax.fori_loop(0, S // rt, norm_body, 0)


def _conv(p, w8, cb, lg, lb):
    n = p.shape[0]
    b = n // S
    vec = pl.BlockSpec((1, CONV_CH), lambda i: (0, 0))
    nslab = CONV_CH // 128
    return pl.pallas_call(
        _conv_kernel,
        grid=(b,),
        in_specs=[
            pl.BlockSpec((S, CONV_CH), lambda i: (i, 3)),
            pl.BlockSpec((S, CONV_CH), lambda i: (i, 4)),
            pl.BlockSpec((CONV_WIDTH, 8, CONV_CH), lambda i: (0, 0, 0)),
            vec, vec, vec,
        ],
        out_specs=pl.BlockSpec((S, CONV_CH), lambda i: (i, 0)),
        out_shape=jax.ShapeDtypeStruct((n, CONV_CH), BF16),
        scratch_shapes=[pltpu.VMEM((nslab, CONV_U_ROWS, 128), F32),
                        pltpu.VMEM((nslab, CONV_V_ROWS, 128), F32)],
        compiler_params=_cparams(("parallel",)),
        name="conv_module",
    )(p, p, w8, cb, lg, lb)


def _attn_kernel(q_ref, k_ref, v_ref, o_ref):
    qi = pl.program_id(2)
    tt = (((1,), (1,)), ((), ()))
    group = ATT_HEADS // ATT_KV

    def run(nk):
        k = k_ref[0:nk, :]
        v = v_ref[0:nk, :]
        for g in range(group):
            sl = slice(g * ATT_HD, (g + 1) * ATT_HD)
            s = lax.dot_general(q_ref[:, sl], k, tt, preferred_element_type=F32)
            m = jnp.max(s, axis=-1, keepdims=True)
            e = jnp.exp(s - m)
            l = jnp.sum(e, axis=-1, keepdims=True)
            o = jnp.dot(e.astype(BF16), v, preferred_element_type=F32)
            o_ref[:, sl] = (o / l).astype(BF16)

    @pl.when(qi == 0)
    def _():
        run(CTX)

    @pl.when(qi > 0)
    def _():
        run(S)


def _attention(p, tq):
    n = p.shape[0]
    b = n // S
    tpb = S // tq
    group = ATT_HEADS // ATT_KV
    gw = group * ATT_HD
    return pl.pallas_call(
        _attn_kernel,
        grid=(b, ATT_KV, tpb),
        in_specs=[
            pl.BlockSpec((tq, gw), lambda i, kv, j: (i * tpb + j, kv)),
            pl.BlockSpec((S, ATT_HD), lambda i, kv, j: (i, ATT_HEADS + kv)),
            pl.BlockSpec((S, ATT_HD), lambda i, kv, j: (i, ATT_HEADS + ATT_KV + kv)),
        ],
        out_specs=pl.BlockSpec((tq, gw), lambda i, kv, j: (i * tpb + j, kv)),
        out_shape=jax.ShapeDtypeStruct((n, ATT_HEADS * ATT_HD), BF16),
        compiler_params=_cparams(("parallel", "parallel", "parallel")),
        name="gq_attention",
    )(p, p, p)


def _outproj_kernel(a_ref, b_ref, x_ref, mod_ref, w_ref, o_ref, *, tm):
    j = pl.program_id(1)
    half = D // 2
    acc = (jnp.dot(a_ref[...], w_ref[0:half, :], preferred_element_type=F32)
           + jnp.dot(b_ref[...], w_ref[half:D, :], preferred_element_type=F32))
    o_ref[...] = x_ref[...] + _row_select(mod_ref, 2, j, tm) * acc


def _outproj(a, a_col, bm, b_col, x, mod, w, tm):
    n = x.shape[0]
    b = n // S
    tpb = S // tm
    half = D // 2
    return pl.pallas_call(
        functools.partial(_outproj_kernel, tm=tm),
        grid=(b, tpb),
        in_specs=[
            pl.BlockSpec((tm, half), lambda i, j: (i * tpb + j, a_col)),
            pl.BlockSpec((tm, half), lambda i, j: (i * tpb + j, b_col)),
            pl.BlockSpec((tm, D), lambda i, j: (i * tpb + j, 0)),
            pl.BlockSpec((None, 2, 6, D), lambda i, j: (i, 0, 0, 0)),
            pl.BlockSpec((D, D), lambda i, j: (0, 0)),
        ],
        out_specs=pl.BlockSpec((tm, D), lambda i, j: (i * tpb + j, 0)),
        out_shape=jax.ShapeDtypeStruct((n, D), F32),
        compiler_params=_cparams(("parallel", "parallel")),
        name="outproj_residual",
    )(a, bm, x, mod, w)


def _router_kernel(x_ref, mod_ref, w_ref, b_ref, hr_ref, info_ref, cnt_ref, run_ref, *, tm):
    i = pl.program_id(0)
    j = pl.program_id(1)

    @pl.when((i == 0) & (j == 0))
    def _():
        run_ref[...] = jnp.zeros_like(run_ref)

    h = _modulated(x_ref[...], mod_ref, j, tm, 3, 4)
    h_hi = h.astype(BF16)
    h_lo = (h - h_hi.astype(F32)).astype(BF16)
    w = w_ref[...]
    pa = jnp.dot(h_hi, w, preferred_element_type=F32)
    pb = jnp.dot(h_lo, w, preferred_element_type=F32)
    logits = pa + pltpu.roll(pa, 128 - 32, axis=1) + pb + b_ref[...]
    lane = lax.broadcasted_iota(jnp.int32, (tm, 128), 1).astype(F32)
    neg = jnp.float32(-jnp.inf)
    big = jnp.float32(1024.0)

    gl = jnp.where(lane < MOE_GROUPS, logits, neg)
    gmax = jnp.max(gl, axis=-1, keepdims=True)
    gidx = jnp.min(jnp.where(gl == gmax, lane, big), axis=-1, keepdims=True)
    gsum = jnp.sum(jnp.exp(gl - gmax), axis=-1, keepdims=True)
    p_group = 1.0 / gsum

    base = MOE_GROUPS + MOE_EPG * gidx
    el = jnp.where((lane >= base) & (lane < base + MOE_EPG), logits, neg)
    emax = jnp.max(el, axis=-1, keepdims=True)
    i1 = jnp.min(jnp.where(el == emax, lane, big), axis=-1, keepdims=True)
    esum = jnp.sum(jnp.exp(el - emax), axis=-1, keepdims=True)
    el2 = jnp.where(lane == i1, neg, el)
    e2max = jnp.max(el2, axis=-1, keepdims=True)
    i2 = jnp.min(jnp.where(el2 == e2max, lane, big), axis=-1, keepdims=True)
    p1 = 1.0 / esum
    p2 = jnp.exp(e2max - emax) / esum
    w1 = p_group * p1 / (p1 + p2)
    w2 = p_group * p2 / (p1 + p2)

    a1 = i1 - base
    a2 = i2 - base
    first_low = a1 < a2
    lo = jnp.where(first_low, a1, a2)
    hi = jnp.where(first_low, a2, a1)
    w_lo = jnp.where(first_low, w1, w2)
    w_hi = jnp.where(first_low, w2, w1)
    pair = jnp.where(lo == 0.0, hi - 1.0, jnp.where(lo == 1.0, hi + 1.0, 5.0))
    cls = gidx * 6.0 + pair

    onehot = jnp.where(lane == cls, 1.0, 0.0)
    r_i = lax.broadcasted_iota(jnp.int32, (tm, tm), 0)
    c_i = lax.broadcasted_iota(jnp.int32, (tm, tm), 1)
    strict = jnp.where(r_i > c_i, 1.0, 0.0).astype(BF16)
    before = jnp.dot(strict, onehot.astype(BF16), preferred_element_type=F32)
    rank = jnp.sum(onehot * (before + run_ref[...]), axis=-1, keepdims=True)
    run_ref[...] = run_ref[...] + jnp.sum(onehot, axis=0, keepdims=True)

    route = jnp.where(lane == 0, cls,
                      jnp.where(lane == 1, rank,
                                jnp.where(lane == 2, w_lo,
                                          jnp.where(lane == 3, w_hi, 0.0))))
    hr_ref[:, 0:D] = h
    hr_ref[:, D:ROUTE_W] = route
    info_ref[...] = jnp.transpose(route)[0:8, :]
    cnt_ref[...] = run_ref[...]


def _router(x, mod, w2, bias, tm):
    n = x.shape[0]
    b = n // S
    tpb = S // tm
    return pl.pallas_call(
        functools.partial(_router_kernel, tm=tm),
        grid=(b, tpb),
        in_specs=[
            pl.BlockSpec((tm, D), lambda i, j: (i * tpb + j, 0)),
            pl.BlockSpec((None, 2, 6, D), lambda i, j: (i, 0, 0, 0)),
            pl.BlockSpec((D, 128), lambda i, j: (0, 0)),
            pl.BlockSpec((1, 128), lambda i, j: (0, 0)),
        ],
        out_specs=[
            pl.BlockSpec((tm, ROUTE_W), lambda i, j: (i * tpb + j, 0)),
            pl.BlockSpec((None, 8, tm), lambda i, j: (i * tpb + j, 0, 0)),
            pl.BlockSpec((1, 128), lambda i, j: (0, 0)),
        ],
        out_shape=[
            jax.ShapeDtypeStruct((n, ROUTE_W), F32),
            jax.ShapeDtypeStruct((n // tm, 8, tm), F32),
            jax.ShapeDtypeStruct((1, 128), F32),
        ],
        scratch_shapes=[pltpu.VMEM((1, 128), F32)],
        compiler_params=_cparams(("arbitrary", "arbitrary")),
        name="moe_router",
    )(x, mod, w2, bias)


def _dispatch_kernel(pos_ref, hr_ref, xs_in_ref, xs_ref, sem, *, td):
    del xs_in_ref

    def body(r8, carry):
        for u in range(DMA_UNROLL):
            r = r8 * DMA_UNROLL + u
            dst = pos_ref[0, 0, r]
            pltpu.make_async_copy(hr_ref.at[pl.ds(r, 1)], xs_ref.at[pl.ds(dst, 1)], sem).start()
        return carry

    lax.fori_loop(0, td // DMA_UNROLL, body, 0)
    pltpu.make_async_copy(hr_ref, xs_ref.at[pl.ds(0, td)], sem).wait()


def _dispatch(pos, hr, xs_init, td):
    n = hr.shape[0]
    return pl.pallas_call(
        functools.partial(_dispatch_kernel, td=td),
        grid=(n // td,),
        in_specs=[
            pl.BlockSpec((1, 1, td), lambda i: (i, 0, 0), memory_space=pltpu.SMEM),
            pl.BlockSpec((td, ROUTE_W), lambda i: (i, 0)),
            pl.BlockSpec(memory_space=pl.ANY),
        ],
        out_specs=pl.BlockSpec(memory_space=pl.ANY),
        out_shape=jax.ShapeDtypeStruct(xs_init.shape, F32),
        scratch_shapes=[pltpu.SemaphoreType.DMA],
        input_output_aliases={2: 0},
        compiler_params=_cparams(("arbitrary",)),
        name="moe_dispatch",
    )(pos.reshape(n // td, 1, td), hr, xs_init)


def _expert_kernel(be_ref, xs_ref, wga_ref, wua_ref, wda_ref, wgb_ref, wub_ref, wdb_ref, ys_ref,
                   wgu_scr, wd_scr):
    x = xs_ref[:, 0:D].astype(BF16)
    w_lo = xs_ref[:, D + 2:D + 3]
    w_hi = xs_ref[:, D + 3:D + 4]

    @pl.when(be_ref[2, pl.program_id(0)] == 1)
    def _():
        wgu_scr[0] = wga_ref[...].astype(BF16)
        wgu_scr[1] = wua_ref[...].astype(BF16)
        wgu_scr[2] = wgb_ref[...].astype(BF16)
        wgu_scr[3] = wub_ref[...].astype(BF16)
        wd_scr[0] = wda_ref[...].astype(BF16)
        wd_scr[1] = wdb_ref[...].astype(BF16)

    def expert(e):
        g = jnp.dot(x, wgu_scr[2 * e], preferred_element_type=F32)
        u = jnp.dot(x, wgu_scr[2 * e + 1], preferred_element_type=F32)
        hdn = (_silu(g) * u).astype(BF16)
        return jnp.dot(hdn, wd_scr[e], preferred_element_type=F32)

    ys_ref[...] = w_lo * expert(0) + w_hi * expert(1)


def _experts(blk_e, xs, weights, tmb):
    npad = xs.shape[0]
    nblk = npad // tmb
    layer, wg, wu, wd = weights

    def wspec(shape, which):
        return pl.BlockSpec((None, None) + shape, lambda i, be: (layer, be[which, i], 0, 0))

    grid_spec = pltpu.PrefetchScalarGridSpec(
        num_scalar_prefetch=1,
        grid=(nblk,),
        in_specs=[
            pl.BlockSpec((tmb, ROUTE_W), lambda i, be: (i, 0)),
            wspec((D, MOE_HIDDEN), 0), wspec((D, MOE_HIDDEN), 0), wspec((MOE_HIDDEN, D), 0),
            wspec((D, MOE_HIDDEN), 1), wspec((D, MOE_HIDDEN), 1), wspec((MOE_HIDDEN, D), 1),
        ],
        out_specs=pl.BlockSpec((tmb, D), lambda i, be: (i, 0)),
        scratch_shapes=[
            pltpu.VMEM((4, D, MOE_HIDDEN), BF16),
            pltpu.VMEM((2, MOE_HIDDEN, D), BF16),
        ],
    )
    return pl.pallas_call(
        _expert_kernel,
        grid_spec=grid_spec,
        out_shape=jax.ShapeDtypeStruct((npad, D), F32),
        compiler_params=_cparams(("arbitrary",)),
        name="moe_experts",
    )(blk_e, xs, wg, wu, wd, wg, wu, wd)


def _combine_kernel(pos_ref, ys_ref, x_ref, mod_ref, fg_ref, o_ref, buf, sem, *, tc, final):
    j = pl.program_id(1)

    def body(r8, carry):
        for u in range(DMA_UNROLL):
            r = r8 * DMA_UNROLL + u
            src = pos_ref[0, 0, r]
            pltpu.make_async_copy(ys_ref.at[pl.ds(src, 1)], buf.at[pl.ds(r, 1)], sem).start()
        return carry

    lax.fori_loop(0, tc // DMA_UNROLL, body, 0)
    pltpu.make_async_copy(ys_ref.at[pl.ds(0, tc)], buf, sem).wait()
    if final:
        o = x_ref[...] + mod_ref[1, 5:6, :] * buf[...]
        ms = jnp.mean(o * o, axis=-1, keepdims=True)
        o = o * lax.rsqrt(ms + EPS) * fg_ref[...]
    else:
        o = x_ref[...] + _row_select(mod_ref, 5, j, tc) * buf[...]
    o_ref[...] = o


def _combine(pos, ys, x, mod, fg, tc, final):
    n = x.shape[0]
    b = n // S
    tpb = S // tc
    joff = CTX // tc if final else 0
    tpo = tpb - joff
    return pl.pallas_call(
        functools.partial(_combine_kernel, tc=tc, final=final),
        grid=(b, tpo),
        in_specs=[
            pl.BlockSpec((1, 1, tc), lambda i, j: (i * tpb + joff + j, 0, 0), memory_space=pltpu.SMEM),
            pl.BlockSpec(memory_space=pl.ANY),
            pl.BlockSpec((tc, D), lambda i, j: (i * tpb + joff + j, 0)),
            pl.BlockSpec((None, 2, 6, D), lambda i, j: (i, 0, 0, 0)),
            pl.BlockSpec((1, D), lambda i, j: (0, 0)),
        ],
        out_specs=pl.BlockSpec((tc, D), lambda i, j: (i * tpo + j, 0)),
        out_shape=jax.ShapeDtypeStruct((b * tpo * tc, D), F32),
        scratch_shapes=[pltpu.VMEM((tc, D), F32), pltpu.SemaphoreType.DMA],
        compiler_params=_cparams(("arbitrary", "arbitrary")),
        name="moe_combine",
    )(pos.reshape(n // tc, 1, tc), ys, x, mod, fg)


_PAIR_LO = np.array([0, 0, 0, 1, 1, 2], np.int32)
_PAIR_HI = np.array([1, 2, 3, 2, 3, 3], np.int32)


def _even_weights(w_in, dec_w_f, dec_b_f, dec_w_b, dec_b_b):
    hk = GLA_HEADS * GLA_DK
    hv = GLA_HEADS * GLA_DV
    o_q, o_k, o_v, o_r = 0, hk, 2 * hk, 2 * hk + hv
    o_zf = o_r + hv
    o_zb = o_zf + GLA_RANK
    o_ca = o_zb + GLA_RANK
    o_cg = o_ca + CONV_CH
    main = jnp.concatenate([w_in[:, o_q:o_zf], w_in[:, o_ca:o_cg + CONV_CH]], axis=1)
    gates = jnp.concatenate([w_in[:, o_zf:o_ca], jnp.zeros((D, 128 - 2 * GLA_RANK), F32)], axis=1)
    w = jnp.concatenate([main, gates], axis=1).astype(BF16)
    wdec = jnp.zeros((128, 2 * hk), F32)
    wdec = wdec.at[0:GLA_RANK, 0:hk].set(dec_w_f).at[GLA_RANK:2 * GLA_RANK, hk:].set(dec_w_b)
    bdec = jnp.concatenate([dec_b_f, dec_b_b]).reshape(1, 2 * hk)
    return w, wdec.astype(BF16), bdec


def _odd_weights(w_in, q_norm_g, k_norm_g):
    perm = np.concatenate([np.arange(0, ATT_HD, 2), np.arange(1, ATT_HD, 2)])
    nqk = ATT_HEADS + ATT_KV
    cols = np.concatenate([h * ATT_HD + perm for h in range(nqk)]
                          + [np.arange(nqk * ATT_HD, ODD_W)])
    w = w_in[:, cols].astype(BF16)
    gq = q_norm_g[perm] * (ATT_HD ** -0.5)
    gain = jnp.concatenate([jnp.tile(gq, ATT_HEADS), jnp.tile(k_norm_g[perm], ATT_KV)]).reshape(1, -1)
    return w, gain


def _rope_tables():
    t = jnp.arange(SEQ)
    row = (t // GRID_W).astype(F32)
    col = (t % GRID_W).astype(F32)
    ppa = ATT_HD // 4
    inv = ROPE_THETA ** (-jnp.arange(ppa, dtype=F32) / ppa)
    ang = jnp.concatenate([row[:, None] * inv, col[:, None] * inv], axis=-1)
    cos = jnp.cos(ang)
    sin = jnp.sin(ang)
    cos_t = jnp.concatenate([jnp.ones((CTX, ATT_HD), F32), jnp.concatenate([cos, cos], axis=1)], axis=0)
    sin_t = jnp.concatenate([jnp.zeros((CTX, ATT_HD), F32), jnp.concatenate([-sin, sin], axis=1)], axis=0)
    return cos_t, sin_t


def _router_weights(w_group, b_group, w_router, b_router):
    wl = jnp.concatenate([w_group, w_router.reshape(D, MOE_EXPERTS)], axis=1)
    hi = wl.astype(BF16)
    lo = (wl - hi.astype(F32)).astype(BF16)
    nl = MOE_GROUPS + MOE_EXPERTS
    w2 = jnp.zeros((D, 128), BF16).at[:, 0:nl].set(hi).at[:, 32:32 + nl].set(lo)
    bias = jnp.zeros((1, 128), F32).at[0, 0:nl].set(jnp.concatenate([b_group, b_router.reshape(-1)]))
    return w2, bias


def _moe(x, mod, w2, bias, weights, fg, xs_buf, final, tm, tmb, td, tc):
    n = x.shape[0]
    hr, info, cnt = _router(x, mod, w2, bias, tm)
    counts = jnp.round(cnt[0, 0:MOE_CLASSES]).astype(jnp.int32)
    padded = ((counts + tmb - 1) // tmb) * tmb
    ends = jnp.cumsum(padded)
    offs = ends - padded
    cls = jnp.round(info[:, 0, :]).astype(jnp.int32)
    rank = jnp.round(info[:, 1, :]).astype(jnp.int32)
    pos = rank
    for k in range(MOE_CLASSES):
        pos = pos + jnp.where(cls == k, offs[k], 0)
    pos = pos.reshape(n)
    npad = n + MOE_CLASSES * tmb
    nblk = npad // tmb
    starts = jnp.arange(nblk, dtype=jnp.int32) * tmb
    blk_cls = jnp.minimum(jnp.sum((ends[None, :] <= starts[:, None]).astype(jnp.int32), axis=1),
                          MOE_CLASSES - 1)
    grp = blk_cls // 6
    pr = blk_cls % 6
    changed = jnp.concatenate([jnp.ones((1,), jnp.int32),
                               (blk_cls[1:] != blk_cls[:-1]).astype(jnp.int32)])
    blk_e = jnp.stack([grp * MOE_EPG + jnp.asarray(_PAIR_LO)[pr],
                       grp * MOE_EPG + jnp.asarray(_PAIR_HI)[pr], changed], axis=0).astype(jnp.int32)
    xs = _dispatch(pos, hr, xs_buf, td)
    ys = _experts(blk_e, xs, weights, tmb)
    return _combine(pos, ys, x, mod, fg, CTX if final else tc, final), xs


def kernel(x, c, ctx, c_ctx, ada_w, ada_b, even_w_in, even_dec_w_f, even_dec_b_f, even_dec_w_b,
           even_dec_b_b, even_gla_norm_g, even_conv_w, even_conv_b, even_conv_norm_g, even_conv_norm_b,
           even_w_out, odd_w_in, odd_q_norm_g, odd_k_norm_g, odd_w_out, moe_w_group, moe_b_group,
           moe_w_router, moe_b_router, moe_w_gate, moe_w_up, moe_w_down, final_norm_g):
    b = x.shape[0]
    n = b * S
    tm = ROUTER_TILE
    tp = PROJ_TILE

    cvec = jnp.zeros((40, D), F32).at[0:b].set(c).at[b].set(c_ctx)
    mods = _mods(cvec, ada_w, ada_b)
    mod_lat = mods[:, 0:b].reshape(DEPTH, b, 1, 6, D)
    mod_ctx = jnp.broadcast_to(mods[:, b].reshape(DEPTH, 1, 1, 6, D), (DEPTH, b, 1, 6, D))
    mod_all = jnp.concatenate([mod_ctx, mod_lat], axis=2)

    xc = jnp.concatenate([ctx, x], axis=1).reshape(n, D)
    cos_t, sin_t = _rope_tables()
    fg = final_norm_g.reshape(1, D)
    xs_buf = jnp.zeros((n + MOE_CLASSES * MOE_BLOCK, ROUTE_W), F32)

    for l in range(DEPTH):
        i = l // 2
        mod = mod_all[l]
        if l % 2 == 0:
            w, wdec, bdec = _even_weights(even_w_in[i], even_dec_w_f[i], even_dec_b_f[i],
                                          even_dec_w_b[i], even_dec_b_b[i])
            p = _inproj(xc, mod, w, EVEN_W, tp)
            og = _gla(p, wdec, bdec, even_gla_norm_g[i].reshape(1, GLA_DV))
            cw8 = jnp.broadcast_to(even_conv_w[i].reshape(CONV_WIDTH, 1, CONV_CH), (CONV_WIDTH, 8, CONV_CH))
            oc = _conv(p, cw8, even_conv_b[i].reshape(1, -1), even_conv_norm_g[i].reshape(1, -1),
                       even_conv_norm_b[i].reshape(1, -1))
            xc = _outproj(og, 0, oc, 0, xc, mod, even_w_out[i].astype(BF16), tp)
        else:
            w, gain = _odd_weights(odd_w_in[i], odd_q_norm_g[i], odd_k_norm_g[i])
            p = _inproj(xc, mod, w, ODD_W, QK_PROJ_TILE, extra=(gain, cos_t, sin_t))
            oa = _attention(p, CTX)
            xc = _outproj(oa, 0, oa, 1, xc, mod, odd_w_out[i].astype(BF16), tp)
        w2, bias = _router_weights(moe_w_group[l], moe_b_group[l], moe_w_router[l], moe_b_router[l])
        xc, xs_buf = _moe(xc, mod, w2, bias, (l, moe_w_gate, moe_w_up, moe_w_down), fg, xs_buf,
                          l == DEPTH - 1, tm, MOE_BLOCK, ROW_DMA_TILE, ROW_DMA_TILE)

    return xc.reshape(b, SEQ, D)
```

```python
import functools

import jax
import jax.numpy as jnp
import numpy as np
from jax import lax
from jax.experimental import pallas as pl
from jax.experimental.pallas import tpu as pltpu

F32 = jnp.float32
BF16 = jnp.bfloat16

D = 1024
CTX = 256
SEQ = 2048
S = CTX + SEQ
DEPTH = 4
GRID_W = 64
EPS = 1e-6
GLA_HEADS = 4
GLA_DV = 128
GLA_DK = 64
GLA_RANK = 16
GLA_NORMALIZER = 16.0
CONV_CH = 512
CONV_WIDTH = 31
ATT_HD = 128
ATT_HEADS = 8
ATT_KV = 2
ROPE_THETA = 10000.0
MOE_GROUPS = 4
MOE_EPG = 4
MOE_EXPERTS = 16
MOE_HIDDEN = 512
MOE_CLASSES = MOE_GROUPS * 6

EVEN_W = 2688
ODD_W = 1536
GLA_CHUNK = 128
GLA_UNROLL = 2
CONV_RT = 64
CONV_STRAND_CTX = 36
CONV_STRAND_LAT = 260
CONV_U_CTX = 16
CONV_U_LAT = CONV_U_CTX + 8 * CONV_STRAND_CTX + 32
CONV_U_ROWS = CONV_U_LAT + 8 * CONV_STRAND_LAT + 16
CONV_V_LAT = 8 * CONV_STRAND_CTX
CONV_V_ROWS = CONV_V_LAT + 8 * CONV_STRAND_LAT
ROUTE_W = D + 128
DMA_UNROLL = 8
MOE_BLOCK = 256
ROUTER_TILE = 256
ROUTE_ROWS = 24
PROJ_TILE = 768
QK_PROJ_TILE = 256
ROW_DMA_TILE = 768
VMEM_LIMIT = 48 * 1024 * 1024


def _cparams(sem):
    return pltpu.CompilerParams(dimension_semantics=sem, vmem_limit_bytes=VMEM_LIMIT)


def _sigmoid(v):
    return 0.5 * jnp.tanh(0.5 * v) + 0.5


def _silu(v):
    return v * _sigmoid(v)


def _mods_kernel(c_ref, w_ref, b_ref, o_ref):
    a = _silu(c_ref[...])
    o_ref[...] = jnp.dot(a, w_ref[...], preferred_element_type=F32,
                         precision=lax.Precision.HIGHEST) + b_ref[...]


def _mods(cvec, ada_w, ada_b):
    nl, _, n6 = ada_w.shape
    rows = cvec.shape[0]
    nb = n6 // D
    return pl.pallas_call(
        _mods_kernel,
        grid=(nl, nb),
        in_specs=[
            pl.BlockSpec((rows, D), lambda l, j: (0, 0)),
            pl.BlockSpec((None, D, D), lambda l, j: (l, 0, j)),
            pl.BlockSpec((None, 1, D), lambda l, j: (l, 0, j)),
        ],
        out_specs=pl.BlockSpec((None, rows, D), lambda l, j: (l, 0, j)),
        out_shape=jax.ShapeDtypeStruct((nl, rows, n6), F32),
        compiler_params=_cparams(("parallel", "parallel")),
        name="adaln_mods",
    )(cvec, ada_w, ada_b.reshape(nl, 1, n6))


def _row_select(mod_ref, idx, j, tm):
    srow = j * tm + lax.broadcasted_iota(jnp.int32, (tm, 1), 0)
    return jnp.where(srow < CTX, mod_ref[0, idx:idx + 1, :], mod_ref[1, idx:idx + 1, :])


def _modulated(x, mod_ref, j, tm, shift_idx, scale_idx):
    ms = jnp.mean(x * x, axis=-1, keepdims=True)
    xn = x * lax.rsqrt(ms + EPS)
    return xn * (1.0 + _row_select(mod_ref, scale_idx, j, tm)) + _row_select(mod_ref, shift_idx, j, tm)


def _inproj_even_kernel(x_ref, mod_ref, w_ref, o_ref, *, tm):
    j = pl.program_id(1)
    h = _modulated(x_ref[...], mod_ref, j, tm, 0, 1)
    o_ref[...] = jnp.dot(h.astype(BF16), w_ref[...], preferred_element_type=F32).astype(BF16)


def _inproj_odd_kernel(x_ref, mod_ref, w_ref, gain_ref, cos_ref, sin_ref, o_ref, *, tm):
    j = pl.program_id(1)
    h = _modulated(x_ref[...], mod_ref, j, tm, 0, 1)
    p = jnp.dot(h.astype(BF16), w_ref[...], preferred_element_type=F32)
    cos = cos_ref[...]
    sin = sin_ref[...]
    nqk = ATT_HEADS + ATT_KV
    for hd in range(nqk):
        sl = slice(hd * ATT_HD, (hd + 1) * ATT_HD)
        ph = p[:, sl]
        ms = jnp.mean(ph * ph, axis=-1, keepdims=True)
        ph = ph * lax.rsqrt(ms + EPS) * gain_ref[:, sl]
        ph = ph * cos + pltpu.roll(ph, ATT_HD // 2, axis=1) * sin
        o_ref[:, sl] = ph.astype(BF16)
    o_ref[:, nqk * ATT_HD:] = p[:, nqk * ATT_HD:].astype(BF16)


def _inproj(x, mod, w, width, tm, extra=None):
    n = x.shape[0]
    b = n // S
    tpb = S // tm
    in_specs = [
        pl.BlockSpec((tm, D), lambda i, j: (i * tpb + j, 0)),
        pl.BlockSpec((None, 2, 6, D), lambda i, j: (i, 0, 0, 0)),
        pl.BlockSpec((D, width), lambda i, j: (0, 0)),
    ]
    args = [x, mod, w]
    if extra is None:
        kern = functools.partial(_inproj_even_kernel, tm=tm)
        name = "inproj_even"
    else:
        gain, cos, sin = extra
        in_specs += [
            pl.BlockSpec((1, (ATT_HEADS + ATT_KV) * ATT_HD), lambda i, j: (0, 0)),
            pl.BlockSpec((tm, ATT_HD), lambda i, j: (j, 0)),
            pl.BlockSpec((tm, ATT_HD), lambda i, j: (j, 0)),
        ]
        args += [gain, cos, sin]
        kern = functools.partial(_inproj_odd_kernel, tm=tm)
        name = "inproj_odd"
    return pl.pallas_call(
        kern,
        grid=(b, tpb),
        in_specs=in_specs,
        out_specs=pl.BlockSpec((tm, width), lambda i, j: (i * tpb + j, 0)),
        out_shape=jax.ShapeDtypeStruct((n, width), BF16),
        compiler_params=_cparams(("parallel", "parallel")),
        name=name,
    )(*args)


def _log_sigmoid(v):
    return jnp.minimum(v, 0.0) - jnp.log(1.0 + jnp.exp(-jnp.abs(v)))


def _split3(v):
    hi = v.astype(BF16)
    r1 = v - hi.astype(F32)
    mid = r1.astype(BF16)
    lo = (r1 - mid.astype(F32)).astype(BF16)
    return hi, mid, lo


def _gla_kernel(qk_ref, v_ref, z_ref, r_ref, wdec_ref, bdec_ref, gn_ref, o_ref, of_ref, ob_ref, st_ref):
    c = GLA_CHUNK
    nch = S // c
    ncc = CTX // c
    hk = GLA_HEADS * GLA_DK
    rows = lax.broadcasted_iota(jnp.int32, (c, c), 0)
    cols = lax.broadcasted_iota(jnp.int32, (c, c), 1)
    tt = (((1,), (1,)), ((), ()))
    ta = (((0,), (0,)), ((), ()))

    def chunk(ci, fwd):
        r0 = pl.multiple_of(ci * c, c)
        qk = qk_ref[pl.ds(r0, c), :]
        v = v_ref[pl.ds(r0, c), :]
        z = z_ref[pl.ds(r0, c), :]
        g0 = 0 if fwd else hk
        gl = jnp.dot(z, wdec_ref[:, g0:g0 + hk], preferred_element_type=F32) + bdec_ref[:, g0:g0 + hk]
        la = _log_sigmoid(gl) * (1.0 / GLA_NORMALIZER)
        keep = (rows >= cols) if fwd else (rows <= cols)
        tri = jnp.where(keep, 1.0, 0.0).astype(BF16)
        hi, mid, lo = _split3(la)
        cum = (jnp.dot(tri, hi, preferred_element_type=F32)
               + jnp.dot(tri, mid, preferred_element_type=F32)
               + jnp.dot(tri, lo, preferred_element_type=F32))
        total = cum[c - 1:c, :] if fwd else cum[0:1, :]
        mref = cum[c // 2:c // 2 + 1, :]
        q = qk[:, 0:hk].astype(F32) * (GLA_DK ** -0.5)
        k = qk[:, hk:2 * hk].astype(F32)
        q_in = (q * jnp.exp(cum - mref)).astype(BF16)
        k_in = (k * jnp.exp(mref - cum)).astype(BF16)
        q_st = (q * jnp.exp(cum)).astype(BF16)
        k_end = (k * jnp.exp(total - cum)).astype(BF16)
        dec = jnp.exp(total)
        for h in range(GLA_HEADS):
            ks = slice(h * GLA_DK, (h + 1) * GLA_DK)
            vs = slice(h * GLA_DV, (h + 1) * GLA_DV)
            att = lax.dot_general(q_in[:, ks], k_in[:, ks], tt, preferred_element_type=F32)
            att = jnp.where(keep, att, 0.0).astype(BF16)
            sref = st_ref.at[0 if fwd else 1]
            st = sref[h]
            o = (jnp.dot(att, v[:, vs], preferred_element_type=F32)
                 + lax.dot_general(q_st[:, ks], st.astype(BF16), tt, preferred_element_type=F32))
            u = lax.dot_general(v[:, vs], k_end[:, ks], ta, preferred_element_type=F32)
            sref[h] = st * dec[:, ks] + u
            dst = of_ref if fwd else ob_ref
            dst[pl.ds(r0, c), vs] = o

    st_ref[...] = jnp.zeros_like(st_ref)

    def scan_body(tu, carry):
        for u in range(GLA_UNROLL):
            t = tu * GLA_UNROLL + u
            chunk(t, True)
            chunk(jnp.where(t < ncc, ncc - 1 - t, nch - 1 - (t - ncc)), False)
        return carry

    lax.fori_loop(0, nch // GLA_UNROLL, scan_body, 0)

    def out_body(t, carry):
        r0 = pl.multiple_of(t * c, c)
        for h in range(GLA_HEADS):
            vs = slice(h * GLA_DV, (h + 1) * GLA_DV)
            o = of_ref[pl.ds(r0, c), vs] + ob_ref[pl.ds(r0, c), vs]
            ms = jnp.mean(o * o, axis=-1, keepdims=True)
            o = o * lax.rsqrt(ms + EPS) * gn_ref[...]
            o_ref[pl.ds(r0, c), vs] = (o * _silu(r_ref[pl.ds(r0, c), vs].astype(F32))).astype(BF16)
        return carry

    lax.fori_loop(0, nch, out_body, 0)


def _gla(p, wdec, bdec, gn):
    n = p.shape[0]
    b = n // S
    hv = GLA_HEADS * GLA_DV
    return pl.pallas_call(
        _gla_kernel,
        grid=(b,),
        in_specs=[
            pl.BlockSpec((S, 512), lambda i: (i, 0)),
            pl.BlockSpec((S, 512), lambda i: (i, 1)),
            pl.BlockSpec((S, 128), lambda i: (i, 20)),
            pl.BlockSpec((S, 512), lambda i: (i, 2)),
            pl.BlockSpec((128, 512), lambda i: (0, 0)),
            pl.BlockSpec((1, 512), lambda i: (0, 0)),
            pl.BlockSpec((1, GLA_DV), lambda i: (0, 0)),
        ],
        out_specs=pl.BlockSpec((S, hv), lambda i: (i, 0)),
        out_shape=jax.ShapeDtypeStruct((n, hv), BF16),
        scratch_shapes=[
            pltpu.VMEM((S, hv), F32),
            pltpu.VMEM((S, hv), F32),
            pltpu.VMEM((2, GLA_HEADS, GLA_DV, GLA_DK), F32),
        ],
        compiler_params=_cparams(("parallel",)),
        name="gla",
    )(p, p, p, p, wdec, bdec, gn)


def _conv_kernel(ca_ref, cg_ref, w_ref, cb_ref, lg_ref, lb_ref, o_ref, u_ref, v_ref):
    half = CONV_WIDTH // 2
    nslab = CONV_CH // 128
    for sl in range(nslab):
        ls = slice(sl * 128, (sl + 1) * 128)
        u_ref[sl, 0:CONV_U_CTX, :] = jnp.zeros((CONV_U_CTX, 128), F32)
        u_ref[sl, CONV_U_CTX + CTX:CONV_U_LAT, :] = jnp.zeros((CONV_U_LAT - CONV_U_CTX - CTX, 128), F32)
        u_ref[sl, CONV_U_LAT + SEQ:CONV_U_ROWS, :] = jnp.zeros((CONV_U_ROWS - CONV_U_LAT - SEQ, 128), F32)
        u_ref[sl, CONV_U_CTX:CONV_U_CTX + CTX, :] = (
            ca_ref[0:CTX, ls].astype(F32) * _sigmoid(cg_ref[0:CTX, ls].astype(F32)))
        u_ref[sl, CONV_U_LAT:CONV_U_LAT + SEQ, :] = (
            ca_ref[CTX:S, ls].astype(F32) * _sigmoid(cg_ref[CTX:S, ls].astype(F32)))

    def segment(ubase, vbase, strand, rb):
        for sl in range(nslab):
            ls = slice(sl * 128, (sl + 1) * 128)

            def body(blk, carry):
                l0 = blk * rb
                row0 = ubase + l0 - half
                win = [u_ref[sl, pl.ds(row0 + i, 8, stride=strand), :] for i in range(rb + 2 * half)]
                accs = [None] * rb
                for j in range(CONV_WIDTH):
                    wj = w_ref[j, :, ls]
                    for i in range(rb):
                        term = win[i + j] * wj
                        accs[i] = term if j == 0 else accs[i] + term
                for i in range(rb):
                    v_ref[sl, pl.ds(vbase + l0 + i, 8, stride=strand), :] = accs[i]
                return carry

            lax.fori_loop(0, strand // rb, body, 0)

    segment(CONV_U_CTX, 0, CONV_STRAND_CTX, 12)
    segment(CONV_U_LAT, CONV_V_LAT, CONV_STRAND_LAT, 10)

    rt = CONV_RT
    ntile_ctx = CTX // rt

    def norm_body(t, carry):
        r0 = pl.multiple_of(t * rt, rt)
        vrow = pl.multiple_of(r0 + jnp.where(t >= ntile_ctx, CONV_V_LAT - CTX, 0), 8)
        ys = [v_ref[sl, pl.ds(vrow, rt), :] + cb_ref[:, sl * 128:(sl + 1) * 128] for sl in range(nslab)]
        mu = jnp.sum(sum(ys), axis=-1, keepdims=True) * (1.0 / CONV_CH)
        dvs = [y - mu for y in ys]
        var = jnp.sum(sum(d * d for d in dvs), axis=-1, keepdims=True) * (1.0 / CONV_CH)
        inv = lax.rsqrt(var + EPS)
        for sl in range(nslab):
            ls = slice(sl * 128, (sl + 1) * 128)
            y = dvs[sl] * inv * lg_ref[:, ls] + lb_ref[:, ls]
            o_ref[pl.ds(r0, rt), ls] = _silu(y).astype(BF16)
        return carry

    lax.fori_loop(0, S // rt, norm_body, 0)


def _conv(p, w8, cb, lg, lb):
    n = p.shape[0]
    b = n // S
    vec = pl.BlockSpec((1, CONV_CH), lambda i: (0, 0))
    nslab = CONV_CH // 128
    return pl.pallas_call(
        _conv_kernel,
        grid=(b,),
        in_specs=[
            pl.BlockSpec((S, CONV_CH), lambda i: (i, 3)),
            pl.BlockSpec((S, CONV_CH), lambda i: (i, 4)),
            pl.BlockSpec((CONV_WIDTH, 8, CONV_CH), lambda i: (0, 0, 0)),
            vec, vec, vec,
        ],
        out_specs=pl.BlockSpec((S, CONV_CH), lambda i: (i, 0)),
        out_shape=jax.ShapeDtypeStruct((n, CONV_CH), BF16),
        scratch_shapes=[pltpu.VMEM((nslab, CONV_U_ROWS, 128), F32),
                        pltpu.VMEM((nslab, CONV_V_ROWS, 128), F32)],
        compiler_params=_cparams(("parallel",)),
        name="conv_module",
    )(p, p, w8, cb, lg, lb)


def _attn_kernel(q_ref, k_ref, v_ref, o_ref):
    qi = pl.program_id(2)
    tt = (((1,), (1,)), ((), ()))
    group = ATT_HEADS // ATT_KV

    def run(nk):
        k = k_ref[0:nk, :]
        v = v_ref[0:nk, :]
        for g in range(group):
            sl = slice(g * ATT_HD, (g + 1) * ATT_HD)
            s = lax.dot_general(q_ref[:, sl], k, tt, preferred_element_type=F32)
            m = jnp.max(s, axis=-1, keepdims=True)
            e = jnp.exp(s - m)
            l = jnp.sum(e, axis=-1, keepdims=True)
            o = jnp.dot(e.astype(BF16), v, preferred_element_type=F32)
            o_ref[:, sl] = (o / l).astype(BF16)

    @pl.when(qi == 0)
    def _():
        run(CTX)

    @pl.when(qi > 0)
    def _():
        run(S)


def _attention(p, tq):
    n = p.shape[0]
    b = n // S
    tpb = S // tq
    group = ATT_HEADS // ATT_KV
    gw = group * ATT_HD
    return pl.pallas_call(
        _attn_kernel,
        grid=(b, ATT_KV, tpb),
        in_specs=[
            pl.BlockSpec((tq, gw), lambda i, kv, j: (i * tpb + j, kv)),
            pl.BlockSpec((S, ATT_HD), lambda i, kv, j: (i, ATT_HEADS + kv)),
            pl.BlockSpec((S, ATT_HD), lambda i, kv, j: (i, ATT_HEADS + ATT_KV + kv)),
        ],
        out_specs=pl.BlockSpec((tq, gw), lambda i, kv, j: (i * tpb + j, kv)),
        out_shape=jax.ShapeDtypeStruct((n, ATT_HEADS * ATT_HD), BF16),
        compiler_params=_cparams(("parallel", "parallel", "parallel")),
        name="gq_attention",
    )(p, p, p)


def _outproj_kernel(a_ref, b_ref, x_ref, mod_ref, w_ref, o_ref, *, tm):
    j = pl.program_id(1)
    half = D // 2
    acc = (jnp.dot(a_ref[...], w_ref[0:half, :], preferred_element_type=F32)
           + jnp.dot(b_ref[...], w_ref[half:D, :], preferred_element_type=F32))
    o_ref[...] = x_ref[...] + _row_select(mod_ref, 2, j, tm) * acc


def _outproj(a, a_col, bm, b_col, x, mod, w, tm):
    n = x.shape[0]
    b = n // S
    tpb = S // tm
    half = D // 2
    return pl.pallas_call(
        functools.partial(_outproj_kernel, tm=tm),
        grid=(b, tpb),
        in_specs=[
            pl.BlockSpec((tm, half), lambda i, j: (i * tpb + j, a_col)),
            pl.BlockSpec((tm, half), lambda i, j: (i * tpb + j, b_col)),
            pl.BlockSpec((tm, D), lambda i, j: (i * tpb + j, 0)),
            pl.BlockSpec((None, 2, 6, D), lambda i, j: (i, 0, 0, 0)),
            pl.BlockSpec((D, D), lambda i, j: (0, 0)),
        ],
        out_specs=pl.BlockSpec((tm, D), lambda i, j: (i * tpb + j, 0)),
        out_shape=jax.ShapeDtypeStruct((n, D), F32),
        compiler_params=_cparams(("parallel", "parallel")),
        name="outproj_residual",
    )(a, bm, x, mod, w)


def _router_kernel(x_ref, mod_ref, w_ref, b_ref, hr_ref, info_ref, cnt_ref, run_ref, tri_ref, *, tm):
    i = pl.program_id(0)
    j = pl.program_id(1)

    @pl.when((i == 0) & (j == 0))
    def _():
        run_ref[...] = jnp.zeros_like(run_ref)
        r_i = lax.broadcasted_iota(jnp.int32, (tm, tm), 0)
        c_i = lax.broadcasted_iota(jnp.int32, (tm, tm), 1)
        tri_ref[...] = jnp.where(r_i < c_i, 1.0, 0.0).astype(BF16)

    h = _modulated(x_ref[...], mod_ref, j, tm, 3, 4)
    h_hi = h.astype(BF16)
    h_lo = (h - h_hi.astype(F32)).astype(BF16)
    w = w_ref[...]
    tt = (((1,), (1,)), ((), ()))
    pa = lax.dot_general(w, h_hi, tt, preferred_element_type=F32)
    pb = lax.dot_general(w, h_lo, tt, preferred_element_type=F32)
    nrow = ROUTE_ROWS
    logits = pa[0:nrow, :] + pa[32:32 + nrow, :] + pb[0:nrow, :] + b_ref[...]
    row = lax.broadcasted_iota(jnp.int32, (nrow, tm), 0).astype(F32)
    neg = jnp.float32(-jnp.inf)
    big = jnp.float32(1024.0)

    gl = jnp.where(row < MOE_GROUPS, logits, neg)
    gmax = jnp.max(gl, axis=0, keepdims=True)
    gidx = jnp.min(jnp.where(gl == gmax, row, big), axis=0, keepdims=True)
    gsum = jnp.sum(jnp.exp(gl - gmax), axis=0, keepdims=True)
    p_group = 1.0 / gsum

    base = MOE_GROUPS + MOE_EPG * gidx
    el = jnp.where((row >= base) & (row < base + MOE_EPG), logits, neg)
    emax = jnp.max(el, axis=0, keepdims=True)
    i1 = jnp.min(jnp.where(el == emax, row, big), axis=0, keepdims=True)
    esum = jnp.sum(jnp.exp(el - emax), axis=0, keepdims=True)
    el2 = jnp.where(row == i1, neg, el)
    e2max = jnp.max(el2, axis=0, keepdims=True)
    i2 = jnp.min(jnp.where(el2 == e2max, row, big), axis=0, keepdims=True)
    p1 = 1.0 / esum
    p2 = jnp.exp(e2max - emax) / esum
    w1 = p_group * p1 / (p1 + p2)
    w2 = p_group * p2 / (p1 + p2)

    a1 = i1 - base
    a2 = i2 - base
    first_low = a1 < a2
    lo = jnp.where(first_low, a1, a2)
    hi = jnp.where(first_low, a2, a1)
    w_lo = jnp.where(first_low, w1, w2)
    w_hi = jnp.where(first_low, w2, w1)
    pair = jnp.where(lo == 0.0, hi - 1.0, jnp.where(lo == 1.0, hi + 1.0, 5.0))
    cls = gidx * 6.0 + pair

    onehot = jnp.where(row == cls, 1.0, 0.0)
    before = jnp.dot(onehot.astype(BF16), tri_ref[...], preferred_element_type=F32)
    rank = jnp.sum(onehot * (before + run_ref[...]), axis=0, keepdims=True)
    run_ref[...] = run_ref[...] + jnp.sum(onehot, axis=1, keepdims=True)

    r8 = lax.broadcasted_iota(jnp.int32, (8, tm), 0)
    info = jnp.where(r8 == 0, cls, jnp.where(r8 == 1, rank, jnp.where(r8 == 2, w_lo, jnp.where(r8 == 3, w_hi, 0.0))))
    info_ref[...] = info
    hr_ref[:, 0:D] = h
    hr_ref[:, D:ROUTE_W] = jnp.transpose(jnp.concatenate([info, jnp.zeros((120, tm), F32)], axis=0))
    cnt_ref[...] = run_ref[...]


def _router(x, mod, w2, bias, tm):
    n = x.shape[0]
    b = n // S
    tpb = S // tm
    return pl.pallas_call(
        functools.partial(_router_kernel, tm=tm),
        grid=(b, tpb),
        in_specs=[
            pl.BlockSpec((tm, D), lambda i, j: (i * tpb + j, 0)),
            pl.BlockSpec((None, 2, 6, D), lambda i, j: (i, 0, 0, 0)),
            pl.BlockSpec((128, D), lambda i, j: (0, 0)),
            pl.BlockSpec((ROUTE_ROWS, 1), lambda i, j: (0, 0)),
        ],
        out_specs=[
            pl.BlockSpec((tm, ROUTE_W), lambda i, j: (i * tpb + j, 0)),
            pl.BlockSpec((None, 8, tm), lambda i, j: (i * tpb + j, 0, 0)),
            pl.BlockSpec((ROUTE_ROWS, 1), lambda i, j: (0, 0)),
        ],
        out_shape=[
            jax.ShapeDtypeStruct((n, ROUTE_W), F32),
            jax.ShapeDtypeStruct((n // tm, 8, tm), F32),
            jax.ShapeDtypeStruct((ROUTE_ROWS, 1), F32),
        ],
        scratch_shapes=[pltpu.VMEM((ROUTE_ROWS, 1), F32), pltpu.VMEM((tm, tm), BF16)],
        compiler_params=_cparams(("arbitrary", "arbitrary")),
        name="moe_router",
    )(x, mod, w2, bias)


def _dispatch_kernel(pos_ref, hr_ref, xs_in_ref, xs_ref, sem, *, td):
    del xs_in_ref

    def body(r8, carry):
        for u in range(DMA_UNROLL):
            r = r8 * DMA_UNROLL + u
            dst = pos_ref[0, 0, r]
            pltpu.make_async_copy(hr_ref.at[pl.ds(r, 1)], xs_ref.at[pl.ds(dst, 1)], sem).start()
        return carry

    lax.fori_loop(0, td // DMA_UNROLL, body, 0)
    pltpu.make_async_copy(hr_ref, xs_ref.at[pl.ds(0, td)], sem).wait()


def _dispatch(pos, hr, xs_init, td):
    n = hr.shape[0]
    return pl.pallas_call(
        functools.partial(_dispatch_kernel, td=td),
        grid=(n // td,),
        in_specs=[
            pl.BlockSpec((1, 1, td), lambda i: (i, 0, 0), memory_space=pltpu.SMEM),
            pl.BlockSpec((td, ROUTE_W), lambda i: (i, 0)),
            pl.BlockSpec(memory_space=pl.ANY),
        ],
        out_specs=pl.BlockSpec(memory_space=pl.ANY),
        out_shape=jax.ShapeDtypeStruct(xs_init.shape, F32),
        scratch_shapes=[pltpu.SemaphoreType.DMA],
        input_output_aliases={2: 0},
        compiler_params=_cparams(("arbitrary",)),
        name="moe_dispatch",
    )(pos.reshape(n // td, 1, td), hr, xs_init)


def _expert_kernel(be_ref, xs_ref, wga_ref, wua_ref, wda_ref, wgb_ref, wub_ref, wdb_ref, ys_ref,
                   wgu_scr, wd_scr):
    x = xs_ref[:, 0:D].astype(BF16)
    w_lo = xs_ref[:, D + 2:D + 3]
    w_hi = xs_ref[:, D + 3:D + 4]

    @pl.when(be_ref[2, pl.program_id(0)] == 1)
    def _():
        wgu_scr[0] = wga_ref[...].astype(BF16)
        wgu_scr[1] = wua_ref[...].astype(BF16)
        wgu_scr[2] = wgb_ref[...].astype(BF16)
        wgu_scr[3] = wub_ref[...].astype(BF16)
        wd_scr[0] = wda_ref[...].astype(BF16)
        wd_scr[1] = wdb_ref[...].astype(BF16)

    def expert(e):
        g = jnp.dot(x, wgu_scr[2 * e], preferred_element_type=F32)
        u = jnp.dot(x, wgu_scr[2 * e + 1], preferred_element_type=F32)
        hdn = (_silu(g) * u).astype(BF16)
        return jnp.dot(hdn, wd_scr[e], preferred_element_type=F32)

    ys_ref[...] = w_lo * expert(0) + w_hi * expert(1)


def _experts(blk_e, xs, weights, tmb):
    npad = xs.shape[0]
    nblk = npad // tmb
    layer, wg, wu, wd = weights

    def wspec(shape, which):
        return pl.BlockSpec((None, None) + shape, lambda i, be: (layer, be[which, i], 0, 0))

    grid_spec = pltpu.PrefetchScalarGridSpec(
        num_scalar_prefetch=1,
        grid=(nblk,),
        in_specs=[
            pl.BlockSpec((tmb, ROUTE_W), lambda i, be: (i, 0)),
            wspec((D, MOE_HIDDEN), 0), wspec((D, MOE_HIDDEN), 0), wspec((MOE_HIDDEN, D), 0),
            wspec((D, MOE_HIDDEN), 1), wspec((D, MOE_HIDDEN), 1), wspec((MOE_HIDDEN, D), 1),
        ],
        out_specs=pl.BlockSpec((tmb, D), lambda i, be: (i, 0)),
        scratch_shapes=[
            pltpu.VMEM((4, D, MOE_HIDDEN), BF16),
            pltpu.VMEM((2, MOE_HIDDEN, D), BF16),
        ],
    )
    return pl.pallas_call(
        _expert_kernel,
        grid_spec=grid_spec,
        out_shape=jax.ShapeDtypeStruct((npad, D), F32),
        compiler_params=_cparams(("arbitrary",)),
        name="moe_experts",
    )(blk_e, xs, wg, wu, wd, wg, wu, wd)


def _combine_kernel(pos_ref, ys_ref, x_ref, mod_ref, fg_ref, o_ref, buf, sem, *, tc, final):
    j = pl.program_id(1)

    def body(r8, carry):
        for u in range(DMA_UNROLL):
            r = r8 * DMA_UNROLL + u
            src = pos_ref[0, 0, r]
            pltpu.make_async_copy(ys_ref.at[pl.ds(src, 1)], buf.at[pl.ds(r, 1)], sem).start()
        return carry

    lax.fori_loop(0, tc // DMA_UNROLL, body, 0)
    pltpu.make_async_copy(ys_ref.at[pl.ds(0, tc)], buf, sem).wait()
    if final:
        o = x_ref[...] + mod_ref[1, 5:6, :] * buf[...]
        ms = jnp.mean(o * o, axis=-1, keepdims=True)
        o = o * lax.rsqrt(ms + EPS) * fg_ref[...]
    else:
        o = x_ref[...] + _row_select(mod_ref, 5, j, tc) * buf[...]
    o_ref[...] = o


def _combine(pos, ys, x, mod, fg, tc, final):
    n = x.shape[0]
    b = n // S
    tpb = S // tc
    joff = CTX // tc if final else 0
    tpo = tpb - joff
    return pl.pallas_call(
        functools.partial(_combine_kernel, tc=tc, final=final),
        grid=(b, tpo),
        in_specs=[
            pl.BlockSpec((1, 1, tc), lambda i, j: (i * tpb + joff + j, 0, 0), memory_space=pltpu.SMEM),
            pl.BlockSpec(memory_space=pl.ANY),
            pl.BlockSpec((tc, D), lambda i, j: (i * tpb + joff + j, 0)),
            pl.BlockSpec((None, 2, 6, D), lambda i, j: (i, 0, 0, 0)),
            pl.BlockSpec((1, D), lambda i, j: (0, 0)),
        ],
        out_specs=pl.BlockSpec((tc, D), lambda i, j: (i * tpo + j, 0)),
        out_shape=jax.ShapeDtypeStruct((b * tpo * tc, D), F32),
        scratch_shapes=[pltpu.VMEM((tc, D), F32), pltpu.SemaphoreType.DMA],
        compiler_params=_cparams(("arbitrary", "arbitrary")),
        name="moe_combine",
    )(pos.reshape(n // tc, 1, tc), ys, x, mod, fg)


_PAIR_LO = np.array([0, 0, 0, 1, 1, 2], np.int32)
_PAIR_HI = np.array([1, 2, 3, 2, 3, 3], np.int32)


def _even_weights(w_in, dec_w_f, dec_b_f, dec_w_b, dec_b_b):
    hk = GLA_HEADS * GLA_DK
    hv = GLA_HEADS * GLA_DV
    o_q, o_k, o_v, o_r = 0, hk, 2 * hk, 2 * hk + hv
    o_zf = o_r + hv
    o_zb = o_zf + GLA_RANK
    o_ca = o_zb + GLA_RANK
    o_cg = o_ca + CONV_CH
    main = jnp.concatenate([w_in[:, o_q:o_zf], w_in[:, o_ca:o_cg + CONV_CH]], axis=1)
    gates = jnp.concatenate([w_in[:, o_zf:o_ca], jnp.zeros((D, 128 - 2 * GLA_RANK), F32)], axis=1)
    w = jnp.concatenate([main, gates], axis=1).astype(BF16)
    wdec = jnp.zeros((128, 2 * hk), F32)
    wdec = wdec.at[0:GLA_RANK, 0:hk].set(dec_w_f).at[GLA_RANK:2 * GLA_RANK, hk:].set(dec_w_b)
    bdec = jnp.concatenate([dec_b_f, dec_b_b]).reshape(1, 2 * hk)
    return w, wdec.astype(BF16), bdec


def _odd_weights(w_in, q_norm_g, k_norm_g):
    perm = np.concatenate([np.arange(0, ATT_HD, 2), np.arange(1, ATT_HD, 2)])
    nqk = ATT_HEADS + ATT_KV
    cols = np.concatenate([h * ATT_HD + perm for h in range(nqk)]
                          + [np.arange(nqk * ATT_HD, ODD_W)])
    w = w_in[:, cols].astype(BF16)
    gq = q_norm_g[perm] * (ATT_HD ** -0.5)
    gain = jnp.concatenate([jnp.tile(gq, ATT_HEADS), jnp.tile(k_norm_g[perm], ATT_KV)]).reshape(1, -1)
    return w, gain


def _rope_tables():
    t = jnp.arange(SEQ)
    row = (t // GRID_W).astype(F32)
    col = (t % GRID_W).astype(F32)
    ppa = ATT_HD // 4
    inv = ROPE_THETA ** (-jnp.arange(ppa, dtype=F32) / ppa)
    ang = jnp.concatenate([row[:, None] * inv, col[:, None] * inv], axis=-1)
    cos = jnp.cos(ang)
    sin = jnp.sin(ang)
    cos_t = jnp.concatenate([jnp.ones((CTX, ATT_HD), F32), jnp.concatenate([cos, cos], axis=1)], axis=0)
    sin_t = jnp.concatenate([jnp.zeros((CTX, ATT_HD), F32), jnp.concatenate([-sin, sin], axis=1)], axis=0)
    return cos_t, sin_t


def _router_weights(w_group, b_group, w_router, b_router):
    wl = jnp.concatenate([w_group, w_router.reshape(D, MOE_EXPERTS)], axis=1).T
    hi = wl.astype(BF16)
    lo = (wl - hi.astype(F32)).astype(BF16)
    nl = MOE_GROUPS + MOE_EXPERTS
    w2 = jnp.zeros((128, D), BF16).at[0:nl].set(hi).at[32:32 + nl].set(lo)
    bias = jnp.zeros((ROUTE_ROWS, 1), F32).at[0:nl, 0].set(jnp.concatenate([b_group, b_router.reshape(-1)]))
    return w2, bias


def _moe(x, mod, w2, bias, weights, fg, xs_buf, final, tm, tmb, td, tc):
    n = x.shape[0]
    hr, info, cnt = _router(x, mod, w2, bias, tm)
    counts = jnp.round(cnt[0:MOE_CLASSES, 0]).astype(jnp.int32)
    padded = ((counts + tmb - 1) // tmb) * tmb
    ends = jnp.cumsum(padded)
    offs = ends - padded
    cls = jnp.round(info[:, 0, :]).astype(jnp.int32)
    rank = jnp.round(info[:, 1, :]).astype(jnp.int32)
    pos = rank
    for k in range(MOE_CLASSES):
        pos = pos + jnp.where(cls == k, offs[k], 0)
    pos = pos.reshape(n)
    npad = n + MOE_CLASSES * tmb
    nblk = npad // tmb
    starts = jnp.arange(nblk, dtype=jnp.int32) * tmb
    blk_cls = jnp.minimum(jnp.sum((ends[None, :] <= starts[:, None]).astype(jnp.int32), axis=1),
                          MOE_CLASSES - 1)
    grp = blk_cls // 6
    pr = blk_cls % 6
    changed = jnp.concatenate([jnp.ones((1,), jnp.int32),
                               (blk_cls[1:] != blk_cls[:-1]).astype(jnp.int32)])
    blk_e = jnp.stack([grp * MOE_EPG + jnp.asarray(_PAIR_LO)[pr],
                       grp * MOE_EPG + jnp.asarray(_PAIR_HI)[pr], changed], axis=0).astype(jnp.int32)
    xs = _dispatch(pos, hr, xs_buf, td)
    ys = _experts(blk_e, xs, weights, tmb)
    return _combine(pos, ys, x, mod, fg, CTX if final else tc, final), xs


def kernel(x, c, ctx, c_ctx, ada_w, ada_b, even_w_in, even_dec_w_f, even_dec_b_f, even_dec_w_b,
           even_dec_b_b, even_gla_norm_g, even_conv_w, even_conv_b, even_conv_norm_g, even_conv_norm_b,
           even_w_out, odd_w_in, odd_q_norm_g, odd_k_norm_g, odd_w_out, moe_w_group, moe_b_group,
           moe_w_router, moe_b_router, moe_w_gate, moe_w_up, moe_w_down, final_norm_g):
    b = x.shape[0]
    n = b * S
    tm = ROUTER_TILE
    tp = PROJ_TILE

    cvec = jnp.zeros((40, D), F32).at[0:b].set(c).at[b].set(c_ctx)
    mods = _mods(cvec, ada_w, ada_b)
    mod_lat = mods[:, 0:b].reshape(DEPTH, b, 1, 6, D)
    mod_ctx = jnp.broadcast_to(mods[:, b].reshape(DEPTH, 1, 1, 6, D), (DEPTH, b, 1, 6, D))
    mod_all = jnp.concatenate([mod_ctx, mod_lat], axis=2)

    xc = jnp.concatenate([ctx, x], axis=1).reshape(n, D)
    cos_t, sin_t = _rope_tables()
    fg = final_norm_g.reshape(1, D)
    xs_buf = jnp.zeros((n + MOE_CLASSES * MOE_BLOCK, ROUTE_W), F32)

    for l in range(DEPTH):
        i = l // 2
        mod = mod_all[l]
        if l % 2 == 0:
            w, wdec, bdec = _even_weights(even_w_in[i], even_dec_w_f[i], even_dec_b_f[i],
                                          even_dec_w_b[i], even_dec_b_b[i])
            p = _inproj(xc, mod, w, EVEN_W, tp)
            og = _gla(p, wdec, bdec, even_gla_norm_g[i].reshape(1, GLA_DV))
            cw8 = jnp.broadcast_to(even_conv_w[i].reshape(CONV_WIDTH, 1, CONV_CH), (CONV_WIDTH, 8, CONV_CH))
            oc = _conv(p, cw8, even_conv_b[i].reshape(1, -1), even_conv_norm_g[i].reshape(1, -1),
                       even_conv_norm_b[i].reshape(1, -1))
            xc = _outproj(og, 0, oc, 0, xc, mod, even_w_out[i].astype(BF16), tp)
        else:
            w, gain = _odd_weights(odd_w_in[i], odd_q_norm_g[i], odd_k_norm_g[i])
            p = _inproj(xc, mod, w, ODD_W, QK_PROJ_TILE, extra=(gain, cos_t, sin_t))
            oa = _attention(p, CTX)
            xc = _outproj(oa, 0, oa, 1, xc, mod, odd_w_out[i].astype(BF16), tp)
        w2, bias = _router_weights(moe_w_group[l], moe_b_group[l], moe_w_router[l], moe_b_router[l])
        xc, xs_buf = _moe(xc, mod, w2, bias, (l, moe_w_gate, moe_w_up, moe_w_down), fg, xs_buf,
                          l == DEPTH - 1, tm, MOE_BLOCK, ROW_DMA_TILE, ROW_DMA_TILE)

    return xc.reshape(b, SEQ, D)
```

```python
import functools

import jax
import jax.numpy as jnp
import numpy as np
from jax import lax
from jax.experimental import pallas as pl
from jax.experimental.pallas import tpu as pltpu

F32 = jnp.float32
BF16 = jnp.bfloat16

D = 1024
CTX = 256
SEQ = 2048
S = CTX + SEQ
DEPTH = 4
GRID_W = 64
EPS = 1e-6
GLA_HEADS = 4
GLA_DV = 128
GLA_DK = 64
GLA_RANK = 16
GLA_NORMALIZER = 16.0
CONV_CH = 512
CONV_WIDTH = 31
ATT_HD = 128
ATT_HEADS = 8
ATT_KV = 2
ROPE_THETA = 10000.0
MOE_GROUPS = 4
MOE_EPG = 4
MOE_EXPERTS = 16
MOE_HIDDEN = 512
MOE_CLASSES = MOE_GROUPS * 6

EVEN_W = 2688
ODD_W = 1536
GLA_CHUNK = 128
GLA_UNROLL = 2
CONV_RT = 64
CONV_STRAND_CTX = 36
CONV_STRAND_LAT = 260
CONV_U_CTX = 16
CONV_U_LAT = CONV_U_CTX + 8 * CONV_STRAND_CTX + 32
CONV_U_ROWS = CONV_U_LAT + 8 * CONV_STRAND_LAT + 16
CONV_V_LAT = 8 * CONV_STRAND_CTX
CONV_V_ROWS = CONV_V_LAT + 8 * CONV_STRAND_LAT
ROUTE_W = D + 128
DMA_UNROLL = 8
MOE_BLOCK = 256
ROUTE_ROWS = 24
PROJ_TILE = 768
QK_PROJ_TILE = 256
ROW_DMA_TILE = 768
VMEM_LIMIT = 48 * 1024 * 1024


def _cparams(sem):
    return pltpu.CompilerParams(dimension_semantics=sem, vmem_limit_bytes=VMEM_LIMIT)


def _sigmoid(v):
    return 0.5 * jnp.tanh(0.5 * v) + 0.5


def _silu(v):
    return v * _sigmoid(v)


def _mods_kernel(c_ref, w_ref, b_ref, o_ref):
    a = _silu(c_ref[...])
    o_ref[...] = jnp.dot(a, w_ref[...], preferred_element_type=F32,
                         precision=lax.Precision.HIGHEST) + b_ref[...]


def _mods(cvec, ada_w, ada_b):
    nl, _, n6 = ada_w.shape
    rows = cvec.shape[0]
    nb = n6 // D
    return pl.pallas_call(
        _mods_kernel,
        grid=(nl, nb),
        in_specs=[
            pl.BlockSpec((rows, D), lambda l, j: (0, 0)),
            pl.BlockSpec((None, D, D), lambda l, j: (l, 0, j)),
            pl.BlockSpec((None, 1, D), lambda l, j: (l, 0, j)),
        ],
        out_specs=pl.BlockSpec((None, rows, D), lambda l, j: (l, 0, j)),
        out_shape=jax.ShapeDtypeStruct((nl, rows, n6), F32),
        compiler_params=_cparams(("parallel", "parallel")),
        name="adaln_mods",
    )(cvec, ada_w, ada_b.reshape(nl, 1, n6))


def _row_select(mod_ref, idx, j, tm):
    srow = j * tm + lax.broadcasted_iota(jnp.int32, (tm, 1), 0)
    return jnp.where(srow < CTX, mod_ref[0, idx:idx + 1, :], mod_ref[1, idx:idx + 1, :])


def _modulated(x, mod_ref, j, tm, shift_idx, scale_idx):
    ms = jnp.mean(x * x, axis=-1, keepdims=True)
    xn = x * lax.rsqrt(ms + EPS)
    return xn * (1.0 + _row_select(mod_ref, scale_idx, j, tm)) + _row_select(mod_ref, shift_idx, j, tm)


def _inproj_even_kernel(x_ref, mod_ref, w_ref, o_ref, *, tm):
    j = pl.program_id(1)
    h = _modulated(x_ref[...], mod_ref, j, tm, 0, 1)
    o_ref[...] = jnp.dot(h.astype(BF16), w_ref[...], preferred_element_type=F32).astype(BF16)


def _inproj_odd_kernel(x_ref, mod_ref, w_ref, gain_ref, cos_ref, sin_ref, o_ref, *, tm):
    j = pl.program_id(1)
    h = _modulated(x_ref[...], mod_ref, j, tm, 0, 1)
    p = jnp.dot(h.astype(BF16), w_ref[...], preferred_element_type=F32)
    cos = cos_ref[...]
    sin = sin_ref[...]
    nqk = ATT_HEADS + ATT_KV
    for hd in range(nqk):
        sl = slice(hd * ATT_HD, (hd + 1) * ATT_HD)
        ph = p[:, sl]
        ms = jnp.mean(ph * ph, axis=-1, keepdims=True)
        ph = ph * lax.rsqrt(ms + EPS) * gain_ref[:, sl]
        ph = ph * cos + pltpu.roll(ph, ATT_HD // 2, axis=1) * sin
        o_ref[:, sl] = ph.astype(BF16)
    o_ref[:, nqk * ATT_HD:] = p[:, nqk * ATT_HD:].astype(BF16)


def _inproj(x, mod, w, width, tm, extra=None):
    n = x.shape[0]
    b = n // S
    tpb = S // tm
    in_specs = [
        pl.BlockSpec((tm, D), lambda i, j: (i * tpb + j, 0)),
        pl.BlockSpec((None, 2, 6, D), lambda i, j: (i, 0, 0, 0)),
        pl.BlockSpec((D, width), lambda i, j: (0, 0)),
    ]
    args = [x, mod, w]
    if extra is None:
        kern = functools.partial(_inproj_even_kernel, tm=tm)
        name = "inproj_even"
    else:
        gain, cos, sin = extra
        in_specs += [
            pl.BlockSpec((1, (ATT_HEADS + ATT_KV) * ATT_HD), lambda i, j: (0, 0)),
            pl.BlockSpec((tm, ATT_HD), lambda i, j: (j, 0)),
            pl.BlockSpec((tm, ATT_HD), lambda i, j: (j, 0)),
        ]
        args += [gain, cos, sin]
        kern = functools.partial(_inproj_odd_kernel, tm=tm)
        name = "inproj_odd"
    return pl.pallas_call(
        kern,
        grid=(b, tpb),
        in_specs=in_specs,
        out_specs=pl.BlockSpec((tm, width), lambda i, j: (i * tpb + j, 0)),
        out_shape=jax.ShapeDtypeStruct((n, width), BF16),
        compiler_params=_cparams(("parallel", "parallel")),
        name=name,
    )(*args)


def _log_sigmoid(v):
    return jnp.minimum(v, 0.0) - jnp.log(1.0 + jnp.exp(-jnp.abs(v)))


def _split3(v):
    hi = v.astype(BF16)
    r1 = v - hi.astype(F32)
    mid = r1.astype(BF16)
    lo = (r1 - mid.astype(F32)).astype(BF16)
    return hi, mid, lo


def _gla_kernel(qk_ref, v_ref, z_ref, r_ref, wdec_ref, bdec_ref, gn_ref, o_ref, of_ref, ob_ref, st_ref):
    c = GLA_CHUNK
    nch = S // c
    ncc = CTX // c
    hk = GLA_HEADS * GLA_DK
    rows = lax.broadcasted_iota(jnp.int32, (c, c), 0)
    cols = lax.broadcasted_iota(jnp.int32, (c, c), 1)
    tt = (((1,), (1,)), ((), ()))
    ta = (((0,), (0,)), ((), ()))

    def chunk(ci, fwd):
        r0 = pl.multiple_of(ci * c, c)
        qk = qk_ref[pl.ds(r0, c), :]
        v = v_ref[pl.ds(r0, c), :]
        z = z_ref[pl.ds(r0, c), :]
        g0 = 0 if fwd else hk
        gl = jnp.dot(z, wdec_ref[:, g0:g0 + hk], preferred_element_type=F32) + bdec_ref[:, g0:g0 + hk]
        la = _log_sigmoid(gl) * (1.0 / GLA_NORMALIZER)
        keep = (rows >= cols) if fwd else (rows <= cols)
        tri = jnp.where(keep, 1.0, 0.0).astype(BF16)
        hi, mid, lo = _split3(la)
        cum = (jnp.dot(tri, hi, preferred_element_type=F32)
               + jnp.dot(tri, mid, preferred_element_type=F32)
               + jnp.dot(tri, lo, preferred_element_type=F32))
        total = cum[c - 1:c, :] if fwd else cum[0:1, :]
        mref = cum[c // 2:c // 2 + 1, :]
        q = qk[:, 0:hk].astype(F32) * (GLA_DK ** -0.5)
        k = qk[:, hk:2 * hk].astype(F32)
        q_in = (q * jnp.exp(cum - mref)).astype(BF16)
        k_in = (k * jnp.exp(mref - cum)).astype(BF16)
        q_st = (q * jnp.exp(cum)).astype(BF16)
        k_end = (k * jnp.exp(total - cum)).astype(BF16)
        dec = jnp.exp(total)
        for h in range(GLA_HEADS):
            ks = slice(h * GLA_DK, (h + 1) * GLA_DK)
            vs = slice(h * GLA_DV, (h + 1) * GLA_DV)
            att = lax.dot_general(q_in[:, ks], k_in[:, ks], tt, preferred_element_type=F32)
            att = jnp.where(keep, att, 0.0).astype(BF16)
            sref = st_ref.at[0 if fwd else 1]
            st = sref[h]
            o = (jnp.dot(att, v[:, vs], preferred_element_type=F32)
                 + lax.dot_general(q_st[:, ks], st.astype(BF16), tt, preferred_element_type=F32))
            u = lax.dot_general(v[:, vs], k_end[:, ks], ta, preferred_element_type=F32)
            sref[h] = st * dec[:, ks] + u
            dst = of_ref if fwd else ob_ref
            dst[pl.ds(r0, c), vs] = o

    st_ref[...] = jnp.zeros_like(st_ref)

    def scan_body(tu, carry):
        for u in range(GLA_UNROLL):
            t = tu * GLA_UNROLL + u
            chunk(t, True)
            chunk(jnp.where(t < ncc, ncc - 1 - t, nch - 1 - (t - ncc)), False)
        return carry

    lax.fori_loop(0, nch // GLA_UNROLL, scan_body, 0)

    def out_body(t, carry):
        r0 = pl.multiple_of(t * c, c)
        for h in range(GLA_HEADS):
            vs = slice(h * GLA_DV, (h + 1) * GLA_DV)
            o = of_ref[pl.ds(r0, c), vs] + ob_ref[pl.ds(r0, c), vs]
            ms = jnp.mean(o * o, axis=-1, keepdims=True)
            o = o * lax.rsqrt(ms + EPS) * gn_ref[...]
            o_ref[pl.ds(r0, c), vs] = (o * _silu(r_ref[pl.ds(r0, c), vs].astype(F32))).astype(BF16)
        return carry

    lax.fori_loop(0, nch, out_body, 0)


def _gla(p, wdec, bdec, gn):
    n = p.shape[0]
    b = n // S
    hv = GLA_HEADS * GLA_DV
    return pl.pallas_call(
        _gla_kernel,
        grid=(b,),
        in_specs=[
            pl.BlockSpec((S, 512), lambda i: (i, 0)),
            pl.BlockSpec((S, 512), lambda i: (i, 1)),
            pl.BlockSpec((S, 128), lambda i: (i, 20)),
            pl.BlockSpec((S, 512), lambda i: (i, 2)),
            pl.BlockSpec((128, 512), lambda i: (0, 0)),
            pl.BlockSpec((1, 512), lambda i: (0, 0)),
            pl.BlockSpec((1, GLA_DV), lambda i: (0, 0)),
        ],
        out_specs=pl.BlockSpec((S, hv), lambda i: (i, 0)),
        out_shape=jax.ShapeDtypeStruct((n, hv), BF16),
        scratch_shapes=[
            pltpu.VMEM((S, hv), F32),
            pltpu.VMEM((S, hv), F32),
            pltpu.VMEM((2, GLA_HEADS, GLA_DV, GLA_DK), F32),
        ],
        compiler_params=_cparams(("parallel",)),
        name="gla",
    )(p, p, p, p, wdec, bdec, gn)


def _conv_kernel(ca_ref, cg_ref, w_ref, cb_ref, lg_ref, lb_ref, o_ref, u_ref, v_ref):
    half = CONV_WIDTH // 2
    nslab = CONV_CH // 128
    for sl in range(nslab):
        ls = slice(sl * 128, (sl + 1) * 128)
        u_ref[sl, 0:CONV_U_CTX, :] = jnp.zeros((CONV_U_CTX, 128), F32)
        u_ref[sl, CONV_U_CTX + CTX:CONV_U_LAT, :] = jnp.zeros((CONV_U_LAT - CONV_U_CTX - CTX, 128), F32)
        u_ref[sl, CONV_U_LAT + SEQ:CONV_U_ROWS, :] = jnp.zeros((CONV_U_ROWS - CONV_U_LAT - SEQ, 128), F32)
        u_ref[sl, CONV_U_CTX:CONV_U_CTX + CTX, :] = (
            ca_ref[0:CTX, ls].astype(F32) * _sigmoid(cg_ref[0:CTX, ls].astype(F32)))
        u_ref[sl, CONV_U_LAT:CONV_U_LAT + SEQ, :] = (
            ca_ref[CTX:S, ls].astype(F32) * _sigmoid(cg_ref[CTX:S, ls].astype(F32)))

    def segment(ubase, vbase, strand, rb):
        for sl in range(nslab):
            ls = slice(sl * 128, (sl + 1) * 128)

            def body(blk, carry):
                l0 = blk * rb
                row0 = ubase + l0 - half
                win = [u_ref[sl, pl.ds(row0 + i, 8, stride=strand), :] for i in range(rb + 2 * half)]
                accs = [None] * rb
                for j in range(CONV_WIDTH):
                    wj = w_ref[j, :, ls]
                    for i in range(rb):
                        term = win[i + j] * wj
                        accs[i] = term if j == 0 else accs[i] + term
                for i in range(rb):
                    v_ref[sl, pl.ds(vbase + l0 + i, 8, stride=strand), :] = accs[i]
                return carry

            lax.fori_loop(0, strand // rb, body, 0)

    segment(CONV_U_CTX, 0, CONV_STRAND_CTX, 12)
    segment(CONV_U_LAT, CONV_V_LAT, CONV_STRAND_LAT, 10)

    rt = CONV_RT
    ntile_ctx = CTX // rt

    def norm_body(t, carry):
        r0 = pl.multiple_of(t * rt, rt)
        vrow = pl.multiple_of(r0 + jnp.where(t >= ntile_ctx, CONV_V_LAT - CTX, 0), 8)
        ys = [v_ref[sl, pl.ds(vrow, rt), :] + cb_ref[:, sl * 128:(sl + 1) * 128] for sl in range(nslab)]
        mu = jnp.sum(sum(ys), axis=-1, keepdims=True) * (1.0 / CONV_CH)
        dvs = [y - mu for y in ys]
        var = jnp.sum(sum(d * d for d in dvs), axis=-1, keepdims=True) * (1.0 / CONV_CH)
        inv = lax.rsqrt(var + EPS)
        for sl in range(nslab):
            ls = slice(sl * 128, (sl + 1) * 128)
            y = dvs[sl] * inv * lg_ref[:, ls] + lb_ref[:, ls]
            o_ref[pl.ds(r0, rt), ls] = _silu(y).astype(BF16)
        return carry

    lax.fori_loop(0, S // rt, norm_body, 0)


def _conv(p, w8, cb, lg, lb):
    n = p.shape[0]
    b = n // S
    vec = pl.BlockSpec((1, CONV_CH), lambda i: (0, 0))
    nslab = CONV_CH // 128
    return pl.pallas_call(
        _conv_kernel,
        grid=(b,),
        in_specs=[
            pl.BlockSpec((S, CONV_CH), lambda i: (i, 3)),
            pl.BlockSpec((S, CONV_CH), lambda i: (i, 4)),
            pl.BlockSpec((CONV_WIDTH, 8, CONV_CH), lambda i: (0, 0, 0)),
            vec, vec, vec,
        ],
        out_specs=pl.BlockSpec((S, CONV_CH), lambda i: (i, 0)),
        out_shape=jax.ShapeDtypeStruct((n, CONV_CH), BF16),
        scratch_shapes=[pltpu.VMEM((nslab, CONV_U_ROWS, 128), F32),
                        pltpu.VMEM((nslab, CONV_V_ROWS, 128), F32)],
        compiler_params=_cparams(("parallel",)),
        name="conv_module",
    )(p, p, w8, cb, lg, lb)


def _attn_kernel(q_ref, k_ref, v_ref, o_ref):
    qi = pl.program_id(1)
    tt = (((1,), (1,)), ((), ()))
    group = ATT_HEADS // ATT_KV

    def run(nk):
        lane = lax.broadcasted_iota(jnp.int32, (nk, ATT_HD), 1)
        ones_col = jnp.where(lane == 0, 1.0, 0.0).astype(BF16)
        for kv in range(ATT_KV):
            ks = slice(kv * ATT_HD, (kv + 1) * ATT_HD)
            k = k_ref[0:nk, ks]
            v = jnp.concatenate([v_ref[0:nk, ks], ones_col], axis=1)
            for g in range(group):
                hd = kv * group + g
                sl = slice(hd * ATT_HD, (hd + 1) * ATT_HD)
                s = lax.dot_general(q_ref[:, sl], k, tt, preferred_element_type=F32)
                m = jnp.max(s, axis=-1, keepdims=True)
                e = jnp.exp((s - m).astype(BF16))
                o = jnp.dot(e, v, preferred_element_type=F32)
                o_ref[:, sl] = (o[:, 0:ATT_HD] / o[:, ATT_HD:ATT_HD + 1]).astype(BF16)

    @pl.when(qi == 0)
    def _():
        run(CTX)

    @pl.when(qi > 0)
    def _():
        run(S)


def _attention(p, tq):
    n = p.shape[0]
    b = n // S
    tpb = S // tq
    qw = ATT_HEADS * ATT_HD
    kw = ATT_KV * ATT_HD
    return pl.pallas_call(
        _attn_kernel,
        grid=(b, tpb),
        in_specs=[
            pl.BlockSpec((tq, qw), lambda i, j: (i * tpb + j, 0)),
            pl.BlockSpec((S, kw), lambda i, j: (i, qw // kw)),
            pl.BlockSpec((S, kw), lambda i, j: (i, qw // kw + 1)),
        ],
        out_specs=pl.BlockSpec((tq, qw), lambda i, j: (i * tpb + j, 0)),
        out_shape=jax.ShapeDtypeStruct((n, qw), BF16),
        compiler_params=_cparams(("parallel", "parallel")),
        name="gq_attention",
    )(p, p, p)


def _route_tile(x, mod_ref, w_ref, b_ref, hr_ref, info_ref, cnt_ref, run_ref, tri_ref, tm):
    i = pl.program_id(0)
    j = pl.program_id(1)

    @pl.when((i == 0) & (j == 0))
    def _():
        run_ref[...] = jnp.zeros_like(run_ref)
        r_i = lax.broadcasted_iota(jnp.int32, (tm, tm), 0)
        c_i = lax.broadcasted_iota(jnp.int32, (tm, tm), 1)
        tri_ref[...] = jnp.where(r_i < c_i, 1.0, 0.0).astype(BF16)

    h = _modulated(x, mod_ref, j, tm, 3, 4)
    h_hi = h.astype(BF16)
    h_lo = (h - h_hi.astype(F32)).astype(BF16)
    w = w_ref[...]
    tt = (((1,), (1,)), ((), ()))
    pa = lax.dot_general(w, h_hi, tt, preferred_element_type=F32)
    pb = lax.dot_general(w, h_lo, tt, preferred_element_type=F32)
    nrow = ROUTE_ROWS
    logits = pa[0:nrow, :] + pa[32:32 + nrow, :] + pb[0:nrow, :] + b_ref[...]
    row = lax.broadcasted_iota(jnp.int32, (nrow, tm), 0).astype(F32)
    neg = jnp.float32(-jnp.inf)
    big = jnp.float32(1024.0)

    gl = jnp.where(row < MOE_GROUPS, logits, neg)
    gmax = jnp.max(gl, axis=0, keepdims=True)
    gidx = jnp.min(jnp.where(gl == gmax, row, big), axis=0, keepdims=True)
    gsum = jnp.sum(jnp.exp(gl - gmax), axis=0, keepdims=True)
    p_group = 1.0 / gsum

    base = MOE_GROUPS + MOE_EPG * gidx
    el = jnp.where((row >= base) & (row < base + MOE_EPG), logits, neg)
    emax = jnp.max(el, axis=0, keepdims=True)
    i1 = jnp.min(jnp.where(el == emax, row, big), axis=0, keepdims=True)
    esum = jnp.sum(jnp.exp(el - emax), axis=0, keepdims=True)
    el2 = jnp.where(row == i1, neg, el)
    e2max = jnp.max(el2, axis=0, keepdims=True)
    i2 = jnp.min(jnp.where(el2 == e2max, row, big), axis=0, keepdims=True)
    p1 = 1.0 / esum
    p2 = jnp.exp(e2max - emax) / esum
    w1 = p_group * p1 / (p1 + p2)
    w2 = p_group * p2 / (p1 + p2)

    a1 = i1 - base
    a2 = i2 - base
    first_low = a1 < a2
    lo = jnp.where(first_low, a1, a2)
    hi = jnp.where(first_low, a2, a1)
    w_lo = jnp.where(first_low, w1, w2)
    w_hi = jnp.where(first_low, w2, w1)
    pair = jnp.where(lo == 0.0, hi - 1.0, jnp.where(lo == 1.0, hi + 1.0, 5.0))
    cls = gidx * 6.0 + pair

    onehot = jnp.where(row == cls, 1.0, 0.0)
    before = jnp.dot(onehot.astype(BF16), tri_ref[...], preferred_element_type=F32)
    rank = jnp.sum(onehot * (before + run_ref[...]), axis=0, keepdims=True)
    run_ref[...] = run_ref[...] + jnp.sum(onehot, axis=1, keepdims=True)

    r8 = lax.broadcasted_iota(jnp.int32, (8, tm), 0)
    info = jnp.where(r8 == 0, cls, jnp.where(r8 == 1, rank, jnp.where(r8 == 2, w_lo, jnp.where(r8 == 3, w_hi, 0.0))))
    info_ref[...] = info
    hr_ref[:, 0:D] = h
    hr_ref[:, D:ROUTE_W] = jnp.transpose(jnp.concatenate([info, jnp.zeros((120, tm), F32)], axis=0))
    cnt_ref[...] = run_ref[...]


def _outproj_router_kernel(a_ref, b_ref, x_ref, mod_ref, w_ref, rw_ref, rb_ref,
                           o_ref, hr_ref, info_ref, cnt_ref, run_ref, tri_ref, *, tm):
    j = pl.program_id(1)
    half = D // 2
    acc = (jnp.dot(a_ref[...], w_ref[0:half, :], preferred_element_type=F32)
           + jnp.dot(b_ref[...], w_ref[half:D, :], preferred_element_type=F32))
    x1 = x_ref[...] + _row_select(mod_ref, 2, j, tm) * acc
    o_ref[...] = x1
    _route_tile(x1, mod_ref, rw_ref, rb_ref, hr_ref, info_ref, cnt_ref, run_ref, tri_ref, tm)


def _outproj_router(a, a_col, bm, b_col, x, mod, w, w2, bias, tm):
    n = x.shape[0]
    b = n // S
    tpb = S // tm
    half = D // 2
    return pl.pallas_call(
        functools.partial(_outproj_router_kernel, tm=tm),
        grid=(b, tpb),
        in_specs=[
            pl.BlockSpec((tm, half), lambda i, j: (i * tpb + j, a_col)),
            pl.BlockSpec((tm, half), lambda i, j: (i * tpb + j, b_col)),
            pl.BlockSpec((tm, D), lambda i, j: (i * tpb + j, 0)),
            pl.BlockSpec((None, 2, 6, D), lambda i, j: (i, 0, 0, 0)),
            pl.BlockSpec((D, D), lambda i, j: (0, 0)),
            pl.BlockSpec((128, D), lambda i, j: (0, 0)),
            pl.BlockSpec((ROUTE_ROWS, 1), lambda i, j: (0, 0)),
        ],
        out_specs=[
            pl.BlockSpec((tm, D), lambda i, j: (i * tpb + j, 0)),
            pl.BlockSpec((tm, ROUTE_W), lambda i, j: (i * tpb + j, 0)),
            pl.BlockSpec((None, 8, tm), lambda i, j: (i * tpb + j, 0, 0)),
            pl.BlockSpec((ROUTE_ROWS, 1), lambda i, j: (0, 0)),
        ],
        out_shape=[
            jax.ShapeDtypeStruct((n, D), F32),
            jax.ShapeDtypeStruct((n, ROUTE_W), F32),
            jax.ShapeDtypeStruct((n // tm, 8, tm), F32),
            jax.ShapeDtypeStruct((ROUTE_ROWS, 1), F32),
        ],
        scratch_shapes=[pltpu.VMEM((ROUTE_ROWS, 1), F32), pltpu.VMEM((tm, tm), BF16)],
        compiler_params=_cparams(("arbitrary", "arbitrary")),
        name="outproj_router",
    )(a, bm, x, mod, w, w2, bias)


def _dispatch_kernel(pos_ref, hr_ref, xs_in_ref, xs_ref, sem, *, td):
    del xs_in_ref

    def body(r8, carry):
        for u in range(DMA_UNROLL):
            r = r8 * DMA_UNROLL + u
            dst = pos_ref[0, 0, r]
            pltpu.make_async_copy(hr_ref.at[pl.ds(r, 1)], xs_ref.at[pl.ds(dst, 1)], sem).start()
        return carry

    lax.fori_loop(0, td // DMA_UNROLL, body, 0)
    pltpu.make_async_copy(hr_ref, xs_ref.at[pl.ds(0, td)], sem).wait()


def _dispatch(pos, hr, xs_init, td):
    n = hr.shape[0]
    return pl.pallas_call(
        functools.partial(_dispatch_kernel, td=td),
        grid=(n // td,),
        in_specs=[
            pl.BlockSpec((1, 1, td), lambda i: (i, 0, 0), memory_space=pltpu.SMEM),
            pl.BlockSpec((td, ROUTE_W), lambda i: (i, 0)),
            pl.BlockSpec(memory_space=pl.ANY),
        ],
        out_specs=pl.BlockSpec(memory_space=pl.ANY),
        out_shape=jax.ShapeDtypeStruct(xs_init.shape, F32),
        scratch_shapes=[pltpu.SemaphoreType.DMA],
        input_output_aliases={2: 0},
        compiler_params=_cparams(("arbitrary",)),
        name="moe_dispatch",
    )(pos.reshape(n // td, 1, td), hr, xs_init)


def _expert_kernel(be_ref, xs_ref, wga_ref, wua_ref, wda_ref, wgb_ref, wub_ref, wdb_ref, ys_ref,
                   wgu_scr, wd_scr):
    x = xs_ref[:, 0:D].astype(BF16)
    w_lo = xs_ref[:, D + 2:D + 3]
    w_hi = xs_ref[:, D + 3:D + 4]

    @pl.when(be_ref[2, pl.program_id(0)] == 1)
    def _():
        wgu_scr[0] = wga_ref[...].astype(BF16)
        wgu_scr[1] = wua_ref[...].astype(BF16)
        wgu_scr[2] = wgb_ref[...].astype(BF16)
        wgu_scr[3] = wub_ref[...].astype(BF16)
        wd_scr[0] = wda_ref[...].astype(BF16)
        wd_scr[1] = wdb_ref[...].astype(BF16)

    def expert(e):
        g = jnp.dot(x, wgu_scr[2 * e], preferred_element_type=F32)
        u = jnp.dot(x, wgu_scr[2 * e + 1], preferred_element_type=F32)
        hdn = (_silu(g) * u).astype(BF16)
        return jnp.dot(hdn, wd_scr[e], preferred_element_type=F32)

    ys_ref[...] = w_lo * expert(0) + w_hi * expert(1)


def _experts(blk_e, xs, weights, tmb):
    npad = xs.shape[0]
    nblk = npad // tmb
    layer, wg, wu, wd = weights

    def wspec(shape, which):
        return pl.BlockSpec((None, None) + shape, lambda i, be: (layer, be[which, i], 0, 0))

    grid_spec = pltpu.PrefetchScalarGridSpec(
        num_scalar_prefetch=1,
        grid=(nblk,),
        in_specs=[
            pl.BlockSpec((tmb, ROUTE_W), lambda i, be: (i, 0)),
            wspec((D, MOE_HIDDEN), 0), wspec((D, MOE_HIDDEN), 0), wspec((MOE_HIDDEN, D), 0),
            wspec((D, MOE_HIDDEN), 1), wspec((D, MOE_HIDDEN), 1), wspec((MOE_HIDDEN, D), 1),
        ],
        out_specs=pl.BlockSpec((tmb, D), lambda i, be: (i, 0)),
        scratch_shapes=[
            pltpu.VMEM((4, D, MOE_HIDDEN), BF16),
            pltpu.VMEM((2, MOE_HIDDEN, D), BF16),
        ],
    )
    return pl.pallas_call(
        _expert_kernel,
        grid_spec=grid_spec,
        out_shape=jax.ShapeDtypeStruct((npad, D), F32),
        compiler_params=_cparams(("arbitrary",)),
        name="moe_experts",
    )(blk_e, xs, wg, wu, wd, wg, wu, wd)


def _combine_kernel(pos_ref, ys_ref, x_ref, mod_ref, fg_ref, o_ref, buf, sem, *, tc, final):
    j = pl.program_id(1)

    def body(r8, carry):
        for u in range(DMA_UNROLL):
            r = r8 * DMA_UNROLL + u
            src = pos_ref[0, 0, r]
            pltpu.make_async_copy(ys_ref.at[pl.ds(src, 1)], buf.at[pl.ds(r, 1)], sem).start()
        return carry

    lax.fori_loop(0, tc // DMA_UNROLL, body, 0)
    pltpu.make_async_copy(ys_ref.at[pl.ds(0, tc)], buf, sem).wait()
    if final:
        o = x_ref[...] + mod_ref[1, 5:6, :] * buf[...]
        ms = jnp.mean(o * o, axis=-1, keepdims=True)
        o = o * lax.rsqrt(ms + EPS) * fg_ref[...]
    else:
        o = x_ref[...] + _row_select(mod_ref, 5, j, tc) * buf[...]
    o_ref[...] = o


def _combine(pos, ys, x, mod, fg, tc, final):
    n = x.shape[0]
    b = n // S
    tpb = S // tc
    joff = CTX // tc if final else 0
    tpo = tpb - joff
    return pl.pallas_call(
        functools.partial(_combine_kernel, tc=tc, final=final),
        grid=(b, tpo),
        in_specs=[
            pl.BlockSpec((1, 1, tc), lambda i, j: (i * tpb + joff + j, 0, 0), memory_space=pltpu.SMEM),
            pl.BlockSpec(memory_space=pl.ANY),
            pl.BlockSpec((tc, D), lambda i, j: (i * tpb + joff + j, 0)),
            pl.BlockSpec((None, 2, 6, D), lambda i, j: (i, 0, 0, 0)),
            pl.BlockSpec((1, D), lambda i, j: (0, 0)),
        ],
        out_specs=pl.BlockSpec((tc, D), lambda i, j: (i * tpo + j, 0)),
        out_shape=jax.ShapeDtypeStruct((b * tpo * tc, D), F32),
        scratch_shapes=[pltpu.VMEM((tc, D), F32), pltpu.SemaphoreType.DMA],
        compiler_params=_cparams(("arbitrary", "arbitrary")),
        name="moe_combine",
    )(pos.reshape(n // tc, 1, tc), ys, x, mod, fg)


_PAIR_LO = np.array([0, 0, 0, 1, 1, 2], np.int32)
_PAIR_HI = np.array([1, 2, 3, 2, 3, 3], np.int32)


def _even_weights(w_in, dec_w_f, dec_b_f, dec_w_b, dec_b_b):
    hk = GLA_HEADS * GLA_DK
    hv = GLA_HEADS * GLA_DV
    o_q, o_k, o_v, o_r = 0, hk, 2 * hk, 2 * hk + hv
    o_zf = o_r + hv
    o_zb = o_zf + GLA_RANK
    o_ca = o_zb + GLA_RANK
    o_cg = o_ca + CONV_CH
    main = jnp.concatenate([w_in[:, o_q:o_zf], w_in[:, o_ca:o_cg + CONV_CH]], axis=1)
    gates = jnp.concatenate([w_in[:, o_zf:o_ca], jnp.zeros((D, 128 - 2 * GLA_RANK), F32)], axis=1)
    w = jnp.concatenate([main, gates], axis=1).astype(BF16)
    wdec = jnp.zeros((128, 2 * hk), F32)
    wdec = wdec.at[0:GLA_RANK, 0:hk].set(dec_w_f).at[GLA_RANK:2 * GLA_RANK, hk:].set(dec_w_b)
    bdec = jnp.concatenate([dec_b_f, dec_b_b]).reshape(1, 2 * hk)
    return w, wdec.astype(BF16), bdec


def _odd_weights(w_in, q_norm_g, k_norm_g):
    perm = np.concatenate([np.arange(0, ATT_HD, 2), np.arange(1, ATT_HD, 2)])
    nqk = ATT_HEADS + ATT_KV
    cols = np.concatenate([h * ATT_HD + perm for h in range(nqk)]
                          + [np.arange(nqk * ATT_HD, ODD_W)])
    w = w_in[:, cols].astype(BF16)
    gq = q_norm_g[perm] * (ATT_HD ** -0.5)
    gain = jnp.concatenate([jnp.tile(gq, ATT_HEADS), jnp.tile(k_norm_g[perm], ATT_KV)]).reshape(1, -1)
    return w, gain


def _rope_tables():
    t = jnp.arange(SEQ)
    row = (t // GRID_W).astype(F32)
    col = (t % GRID_W).astype(F32)
    ppa = ATT_HD // 4
    inv = ROPE_THETA ** (-jnp.arange(ppa, dtype=F32) / ppa)
    ang = jnp.concatenate([row[:, None] * inv, col[:, None] * inv], axis=-1)
    cos = jnp.cos(ang)
    sin = jnp.sin(ang)
    cos_t = jnp.concatenate([jnp.ones((CTX, ATT_HD), F32), jnp.concatenate([cos, cos], axis=1)], axis=0)
    sin_t = jnp.concatenate([jnp.zeros((CTX, ATT_HD), F32), jnp.concatenate([-sin, sin], axis=1)], axis=0)
    return cos_t, sin_t


def _router_weights(w_group, b_group, w_router, b_router):
    wl = jnp.concatenate([w_group, w_router.reshape(D, MOE_EXPERTS)], axis=1).T
    hi = wl.astype(BF16)
    lo = (wl - hi.astype(F32)).astype(BF16)
    nl = MOE_GROUPS + MOE_EXPERTS
    w2 = jnp.zeros((128, D), BF16).at[0:nl].set(hi).at[32:32 + nl].set(lo)
    bias = jnp.zeros((ROUTE_ROWS, 1), F32).at[0:nl, 0].set(jnp.concatenate([b_group, b_router.reshape(-1)]))
    return w2, bias


def _moe(x, routed, mod, weights, fg, xs_buf, final, tmb, td, tc):
    n = x.shape[0]
    hr, info, cnt = routed
    counts = jnp.round(cnt[0:MOE_CLASSES, 0]).astype(jnp.int32)
    padded = ((counts + tmb - 1) // tmb) * tmb
    ends = jnp.cumsum(padded)
    offs = ends - padded
    cls = jnp.round(info[:, 0, :]).astype(jnp.int32)
    rank = jnp.round(info[:, 1, :]).astype(jnp.int32)
    pos = rank
    for k in range(MOE_CLASSES):
        pos = pos + jnp.where(cls == k, offs[k], 0)
    pos = pos.reshape(n)
    npad = n + MOE_CLASSES * tmb
    nblk = npad // tmb
    starts = jnp.arange(nblk, dtype=jnp.int32) * tmb
    blk_cls = jnp.minimum(jnp.sum((ends[None, :] <= starts[:, None]).astype(jnp.int32), axis=1),
                          MOE_CLASSES - 1)
    grp = blk_cls // 6
    pr = blk_cls % 6
    changed = jnp.concatenate([jnp.ones((1,), jnp.int32),
                               (blk_cls[1:] != blk_cls[:-1]).astype(jnp.int32)])
    blk_e = jnp.stack([grp * MOE_EPG + jnp.asarray(_PAIR_LO)[pr],
                       grp * MOE_EPG + jnp.asarray(_PAIR_HI)[pr], changed], axis=0).astype(jnp.int32)
    xs = _dispatch(pos, hr, xs_buf, td)
    ys = _experts(blk_e, xs, weights, tmb)
    return _combine(pos, ys, x, mod, fg, CTX if final else tc, final), xs


def kernel(x, c, ctx, c_ctx, ada_w, ada_b, even_w_in, even_dec_w_f, even_dec_b_f, even_dec_w_b,
           even_dec_b_b, even_gla_norm_g, even_conv_w, even_conv_b, even_conv_norm_g, even_conv_norm_b,
           even_w_out, odd_w_in, odd_q_norm_g, odd_k_norm_g, odd_w_out, moe_w_group, moe_b_group,
           moe_w_router, moe_b_router, moe_w_gate, moe_w_up, moe_w_down, final_norm_g):
    b = x.shape[0]
    n = b * S
    tp = PROJ_TILE

    cvec = jnp.zeros((40, D), F32).at[0:b].set(c).at[b].set(c_ctx)
    mods = _mods(cvec, ada_w, ada_b)
    mod_lat = mods[:, 0:b].reshape(DEPTH, b, 1, 6, D)
    mod_ctx = jnp.broadcast_to(mods[:, b].reshape(DEPTH, 1, 1, 6, D), (DEPTH, b, 1, 6, D))
    mod_all = jnp.concatenate([mod_ctx, mod_lat], axis=2)

    xc = jnp.concatenate([ctx, x], axis=1).reshape(n, D)
    cos_t, sin_t = _rope_tables()
    fg = final_norm_g.reshape(1, D)
    xs_buf = jnp.zeros((n + MOE_CLASSES * MOE_BLOCK, ROUTE_W), F32)

    for l in range(DEPTH):
        i = l // 2
        mod = mod_all[l]
        w2, bias = _router_weights(moe_w_group[l], moe_b_group[l], moe_w_router[l], moe_b_router[l])
        if l % 2 == 0:
            w, wdec, bdec = _even_weights(even_w_in[i], even_dec_w_f[i], even_dec_b_f[i],
                                          even_dec_w_b[i], even_dec_b_b[i])
            p = _inproj(xc, mod, w, EVEN_W, tp)
            og = _gla(p, wdec, bdec, even_gla_norm_g[i].reshape(1, GLA_DV))
            cw8 = jnp.broadcast_to(even_conv_w[i].reshape(CONV_WIDTH, 1, CONV_CH), (CONV_WIDTH, 8, CONV_CH))
            oc = _conv(p, cw8, even_conv_b[i].reshape(1, -1), even_conv_norm_g[i].reshape(1, -1),
                       even_conv_norm_b[i].reshape(1, -1))
            xc, *routed = _outproj_router(og, 0, oc, 0, xc, mod, even_w_out[i].astype(BF16), w2, bias, tp)
        else:
            w, gain = _odd_weights(odd_w_in[i], odd_q_norm_g[i], odd_k_norm_g[i])
            p = _inproj(xc, mod, w, ODD_W, QK_PROJ_TILE, extra=(gain, cos_t, sin_t))
            oa = _attention(p, CTX)
            xc, *routed = _outproj_router(oa, 0, oa, 1, xc, mod, odd_w_out[i].astype(BF16), w2, bias, tp)
        xc, xs_buf = _moe(xc, routed, mod, (l, moe_w_gate, moe_w_up, moe_w_down), fg, xs_buf,
                          l == DEPTH - 1, MOE_BLOCK, ROW_DMA_TILE, ROW_DMA_TILE)

    return xc.reshape(b, SEQ, D)
```

```python
import functools

import jax
import jax.numpy as jnp
import numpy as np
from jax import lax
from jax.experimental import pallas as pl
from jax.experimental.pallas import tpu as pltpu

F32 = jnp.float32
BF16 = jnp.bfloat16

D = 1024
CTX = 256
SEQ = 2048
S = CTX + SEQ
DEPTH = 4
GRID_W = 64
EPS = 1e-6
GLA_HEADS = 4
GLA_DV = 128
GLA_DK = 64
GLA_RANK = 16
GLA_NORMALIZER = 16.0
CONV_CH = 512
CONV_WIDTH = 31
ATT_HD = 128
ATT_HEADS = 8
ATT_KV = 2
ROPE_THETA = 10000.0
MOE_GROUPS = 4
MOE_EPG = 4
MOE_EXPERTS = 16
MOE_HIDDEN = 512
MOE_CLASSES = MOE_GROUPS * 6

EVEN_W = 2688
ODD_W = 1536
GLA_CHUNK = 128
GLA_UNROLL = 2
CONV_RT = 64
CONV_STRAND_CTX = 36
CONV_STRAND_LAT = 260
CONV_U_CTX = 16
CONV_U_LAT = CONV_U_CTX + 8 * CONV_STRAND_CTX + 32
CONV_U_ROWS = CONV_U_LAT + 8 * CONV_STRAND_LAT + 16
CONV_V_LAT = 8 * CONV_STRAND_CTX
CONV_V_ROWS = CONV_V_LAT + 8 * CONV_STRAND_LAT
ROUTE_W = D + 128
SUBLANES = 8
MOE_BLOCK = 256
ROUTE_ROWS = 24
PROJ_TILE = 768
QK_PROJ_TILE = 256
DISPATCH_TILE = S
COMBINE_TILE = S // 2
VMEM_LIMIT = 48 * 1024 * 1024


def _cparams(sem):
    return pltpu.CompilerParams(dimension_semantics=sem, vmem_limit_bytes=VMEM_LIMIT)


def _sigmoid(v):
    return 0.5 * jnp.tanh(0.5 * v) + 0.5


def _silu(v):
    return v * _sigmoid(v)


def _mods_kernel(c_ref, w_ref, b_ref, o_ref):
    a = _silu(c_ref[...])
    o_ref[...] = jnp.dot(a, w_ref[...], preferred_element_type=F32,
                         precision=lax.Precision.HIGHEST) + b_ref[...]


def _mods(cvec, ada_w, ada_b):
    nl, _, n6 = ada_w.shape
    rows = cvec.shape[0]
    nb = n6 // D
    return pl.pallas_call(
        _mods_kernel,
        grid=(nl, nb),
        in_specs=[
            pl.BlockSpec((rows, D), lambda l, j: (0, 0)),
            pl.BlockSpec((None, D, D), lambda l, j: (l, 0, j)),
            pl.BlockSpec((None, 1, D), lambda l, j: (l, 0, j)),
        ],
        out_specs=pl.BlockSpec((None, rows, D), lambda l, j: (l, 0, j)),
        out_shape=jax.ShapeDtypeStruct((nl, rows, n6), F32),
        compiler_params=_cparams(("parallel", "parallel")),
        name="adaln_mods",
    )(cvec, ada_w, ada_b.reshape(nl, 1, n6))


def _row_select(mod_ref, idx, j, tm):
    srow = j * tm + lax.broadcasted_iota(jnp.int32, (tm, 1), 0)
    return jnp.where(srow < CTX, mod_ref[0, idx:idx + 1, :], mod_ref[1, idx:idx + 1, :])


def _modulated(x, mod_ref, j, tm, shift_idx, scale_idx):
    ms = jnp.mean(x * x, axis=-1, keepdims=True)
    xn = x * lax.rsqrt(ms + EPS)
    return xn * (1.0 + _row_select(mod_ref, scale_idx, j, tm)) + _row_select(mod_ref, shift_idx, j, tm)


def _inproj_even_kernel(x_ref, mod_ref, w_ref, o_ref, *, tm):
    j = pl.program_id(1)
    h = _modulated(x_ref[...], mod_ref, j, tm, 0, 1)
    o_ref[...] = jnp.dot(h.astype(BF16), w_ref[...], preferred_element_type=F32).astype(BF16)


def _inproj_odd_kernel(x_ref, mod_ref, w_ref, gain_ref, cos_ref, sin_ref, o_ref, *, tm):
    j = pl.program_id(1)
    h = _modulated(x_ref[...], mod_ref, j, tm, 0, 1)
    p = jnp.dot(h.astype(BF16), w_ref[...], preferred_element_type=F32)
    cos = cos_ref[...]
    sin = sin_ref[...]
    nqk = ATT_HEADS + ATT_KV
    for hd in range(nqk):
        sl = slice(hd * ATT_HD, (hd + 1) * ATT_HD)
        ph = p[:, sl]
        ms = jnp.mean(ph * ph, axis=-1, keepdims=True)
        ph = ph * lax.rsqrt(ms + EPS) * gain_ref[:, sl]
        ph = ph * cos + pltpu.roll(ph, ATT_HD // 2, axis=1) * sin
        o_ref[:, sl] = ph.astype(BF16)
    o_ref[:, nqk * ATT_HD:] = p[:, nqk * ATT_HD:].astype(BF16)


def _inproj(x, mod, w, width, tm, extra=None):
    n = x.shape[0]
    b = n // S
    tpb = S // tm
    in_specs = [
        pl.BlockSpec((tm, D), lambda i, j: (i * tpb + j, 0)),
        pl.BlockSpec((None, 2, 6, D), lambda i, j: (i, 0, 0, 0)),
        pl.BlockSpec((D, width), lambda i, j: (0, 0)),
    ]
    args = [x, mod, w]
    if extra is None:
        kern = functools.partial(_inproj_even_kernel, tm=tm)
        name = "inproj_even"
    else:
        gain, cos, sin = extra
        in_specs += [
            pl.BlockSpec((1, (ATT_HEADS + ATT_KV) * ATT_HD), lambda i, j: (0, 0)),
            pl.BlockSpec((tm, ATT_HD), lambda i, j: (j, 0)),
            pl.BlockSpec((tm, ATT_HD), lambda i, j: (j, 0)),
        ]
        args += [gain, cos, sin]
        kern = functools.partial(_inproj_odd_kernel, tm=tm)
        name = "inproj_odd"
    return pl.pallas_call(
        kern,
        grid=(b, tpb),
        in_specs=in_specs,
        out_specs=pl.BlockSpec((tm, width), lambda i, j: (i * tpb + j, 0)),
        out_shape=jax.ShapeDtypeStruct((n, width), BF16),
        compiler_params=_cparams(("parallel", "parallel")),
        name=name,
    )(*args)


def _log_sigmoid(v):
    return jnp.minimum(v, 0.0) - jnp.log(1.0 + jnp.exp(-jnp.abs(v)))


def _split3(v):
    hi = v.astype(BF16)
    r1 = v - hi.astype(F32)
    mid = r1.astype(BF16)
    lo = (r1 - mid.astype(F32)).astype(BF16)
    return hi, mid, lo


def _gla_kernel(qk_ref, v_ref, z_ref, r_ref, wdec_ref, bdec_ref, gn_ref, o_ref, of_ref, ob_ref, st_ref):
    c = GLA_CHUNK
    nch = S // c
    ncc = CTX // c
    hk = GLA_HEADS * GLA_DK
    rows = lax.broadcasted_iota(jnp.int32, (c, c), 0)
    cols = lax.broadcasted_iota(jnp.int32, (c, c), 1)
    tt = (((1,), (1,)), ((), ()))
    ta = (((0,), (0,)), ((), ()))

    def chunk(ci, fwd):
        r0 = pl.multiple_of(ci * c, c)
        qk = qk_ref[pl.ds(r0, c), :]
        v = v_ref[pl.ds(r0, c), :]
        z = z_ref[pl.ds(r0, c), :]
        g0 = 0 if fwd else hk
        gl = jnp.dot(z, wdec_ref[:, g0:g0 + hk], preferred_element_type=F32) + bdec_ref[:, g0:g0 + hk]
        la = _log_sigmoid(gl) * (1.0 / GLA_NORMALIZER)
        keep = (rows >= cols) if fwd else (rows <= cols)
        tri = jnp.where(keep, 1.0, 0.0).astype(BF16)
        hi, mid, lo = _split3(la)
        cum = (jnp.dot(tri, hi, preferred_element_type=F32)
               + jnp.dot(tri, mid, preferred_element_type=F32)
               + jnp.dot(tri, lo, preferred_element_type=F32))
        total = cum[c - 1:c, :] if fwd else cum[0:1, :]
        mref = cum[c // 2:c // 2 + 1, :]
        q = qk[:, 0:hk].astype(F32) * (GLA_DK ** -0.5)
        k = qk[:, hk:2 * hk].astype(F32)
        q_in = (q * jnp.exp(cum - mref)).astype(BF16)
        k_in = (k * jnp.exp(mref - cum)).astype(BF16)
        q_st = (q * jnp.exp(cum)).astype(BF16)
        k_end = (k * jnp.exp(total - cum)).astype(BF16)
        dec = jnp.exp(total)
        for h in range(GLA_HEADS):
            ks = slice(h * GLA_DK, (h + 1) * GLA_DK)
            vs = slice(h * GLA_DV, (h + 1) * GLA_DV)
            att = lax.dot_general(q_in[:, ks], k_in[:, ks], tt, preferred_element_type=F32)
            att = jnp.where(keep, att, 0.0).astype(BF16)
            sref = st_ref.at[0 if fwd else 1]
            st = sref[h]
            o = (jnp.dot(att, v[:, vs], preferred_element_type=F32)
                 + lax.dot_general(q_st[:, ks], st.astype(BF16), tt, preferred_element_type=F32))
            u = lax.dot_general(v[:, vs], k_end[:, ks], ta, preferred_element_type=F32)
            sref[h] = st * dec[:, ks] + u
            dst = of_ref if fwd else ob_ref
            dst[pl.ds(r0, c), vs] = o

    st_ref[...] = jnp.zeros_like(st_ref)

    def scan_body(tu, carry):
        for u in range(GLA_UNROLL):
            t = tu * GLA_UNROLL + u
            chunk(t, True)
            chunk(jnp.where(t < ncc, ncc - 1 - t, nch - 1 - (t - ncc)), False)
        return carry

    lax.fori_loop(0, nch // GLA_UNROLL, scan_body, 0)

    def out_body(t, carry):
        r0 = pl.multiple_of(t * c, c)
        for h in range(GLA_HEADS):
            vs = slice(h * GLA_DV, (h + 1) * GLA_DV)
            o = of_ref[pl.ds(r0, c), vs] + ob_ref[pl.ds(r0, c), vs]
            ms = jnp.mean(o * o, axis=-1, keepdims=True)
            o = o * lax.rsqrt(ms + EPS) * gn_ref[...]
            o_ref[pl.ds(r0, c), vs] = (o * _silu(r_ref[pl.ds(r0, c), vs].astype(F32))).astype(BF16)
        return carry

    lax.fori_loop(0, nch, out_body, 0)


def _gla(p, wdec, bdec, gn):
    n = p.shape[0]
    b = n // S
    hv = GLA_HEADS * GLA_DV
    return pl.pallas_call(
        _gla_kernel,
        grid=(b,),
        in_specs=[
            pl.BlockSpec((S, 512), lambda i: (i, 0)),
            pl.BlockSpec((S, 512), lambda i: (i, 1)),
            pl.BlockSpec((S, 128), lambda i: (i, 20)),
            pl.BlockSpec((S, 512), lambda i: (i, 2)),
            pl.BlockSpec((128, 512), lambda i: (0, 0)),
            pl.BlockSpec((1, 512), lambda i: (0, 0)),
            pl.BlockSpec((1, GLA_DV), lambda i: (0, 0)),
        ],
        out_specs=pl.BlockSpec((S, hv), lambda i: (i, 0)),
        out_shape=jax.ShapeDtypeStruct((n, hv), BF16),
        scratch_shapes=[
            pltpu.VMEM((S, hv), F32),
            pltpu.VMEM((S, hv), F32),
            pltpu.VMEM((2, GLA_HEADS, GLA_DV, GLA_DK), F32),
        ],
        compiler_params=_cparams(("parallel",)),
        name="gla",
    )(p, p, p, p, wdec, bdec, gn)


def _conv_kernel(ca_ref, cg_ref, w_ref, cb_ref, lg_ref, lb_ref, o_ref, u_ref, v_ref):
    half = CONV_WIDTH // 2
    nslab = CONV_CH // 128
    for sl in range(nslab):
        ls = slice(sl * 128, (sl + 1) * 128)
        u_ref[sl, 0:CONV_U_CTX, :] = jnp.zeros((CONV_U_CTX, 128), F32)
        u_ref[sl, CONV_U_CTX + CTX:CONV_U_LAT, :] = jnp.zeros((CONV_U_LAT - CONV_U_CTX - CTX, 128), F32)
        u_ref[sl, CONV_U_LAT + SEQ:CONV_U_ROWS, :] = jnp.zeros((CONV_U_ROWS - CONV_U_LAT - SEQ, 128), F32)
        u_ref[sl, CONV_U_CTX:CONV_U_CTX + CTX, :] = (
            ca_ref[0:CTX, ls].astype(F32) * _sigmoid(cg_ref[0:CTX, ls].astype(F32)))
        u_ref[sl, CONV_U_LAT:CONV_U_LAT + SEQ, :] = (
            ca_ref[CTX:S, ls].astype(F32) * _sigmoid(cg_ref[CTX:S, ls].astype(F32)))

    def segment(ubase, vbase, strand, rb):
        for sl in range(nslab):
            ls = slice(sl * 128, (sl + 1) * 128)

            def body(blk, carry):
                l0 = blk * rb
                row0 = ubase + l0 - half
                win = [u_ref[sl, pl.ds(row0 + i, 8, stride=strand), :] for i in range(rb + 2 * half)]
                accs = [None] * rb
                for j in range(CONV_WIDTH):
                    wj = w_ref[j, :, ls]
                    for i in range(rb):
                        term = win[i + j] * wj
                        accs[i] = term if j == 0 else accs[i] + term
                for i in range(rb):
                    v_ref[sl, pl.ds(vbase + l0 + i, 8, stride=strand), :] = accs[i]
                return carry

            lax.fori_loop(0, strand // rb, body, 0)

    segment(CONV_U_CTX, 0, CONV_STRAND_CTX, 12)
    segment(CONV_U_LAT, CONV_V_LAT, CONV_STRAND_LAT, 10)

    rt = CONV_RT
    ntile_ctx = CTX // rt

    def norm_body(t, carry):
        r0 = pl.multiple_of(t * rt, rt)
        vrow = pl.multiple_of(r0 + jnp.where(t >= ntile_ctx, CONV_V_LAT - CTX, 0), 8)
        ys = [v_ref[sl, pl.ds(vrow, rt), :] + cb_ref[:, sl * 128:(sl + 1) * 128] for sl in range(nslab)]
        mu = jnp.sum(sum(ys), axis=-1, keepdims=True) * (1.0 / CONV_CH)
        dvs = [y - mu for y in ys]
        var = jnp.sum(sum(d * d for d in dvs), axis=-1, keepdims=True) * (1.0 / CONV_CH)
        inv = lax.rsqrt(var + EPS)
        for sl in range(nslab):
            ls = slice(sl * 128, (sl + 1) * 128)
            y = dvs[sl] * inv * lg_ref[:, ls] + lb_ref[:, ls]
            o_ref[pl.ds(r0, rt), ls] = _silu(y).astype(BF16)
        return carry

    lax.fori_loop(0, S // rt, norm_body, 0)


def _conv(p, w8, cb, lg, lb):
    n = p.shape[0]
    b = n // S
    vec = pl.BlockSpec((1, CONV_CH), lambda i: (0, 0))
    nslab = CONV_CH // 128
    return pl.pallas_call(
        _conv_kernel,
        grid=(b,),
        in_specs=[
            pl.BlockSpec((S, CONV_CH), lambda i: (i, 3)),
            pl.BlockSpec((S, CONV_CH), lambda i: (i, 4)),
            pl.BlockSpec((CONV_WIDTH, 8, CONV_CH), lambda i: (0, 0, 0)),
            vec, vec, vec,
        ],
        out_specs=pl.BlockSpec((S, CONV_CH), lambda i: (i, 0)),
        out_shape=jax.ShapeDtypeStruct((n, CONV_CH), BF16),
        scratch_shapes=[pltpu.VMEM((nslab, CONV_U_ROWS, 128), F32),
                        pltpu.VMEM((nslab, CONV_V_ROWS, 128), F32)],
        compiler_params=_cparams(("parallel",)),
        name="conv_module",
    )(p, p, w8, cb, lg, lb)


def _attn_kernel(q_ref, k_ref, v_ref, o_ref):
    qi = pl.program_id(1)
    tt = (((1,), (1,)), ((), ()))
    group = ATT_HEADS // ATT_KV

    def run(nk):
        lane = lax.broadcasted_iota(jnp.int32, (nk, ATT_HD), 1)
        ones_col = jnp.where(lane == 0, 1.0, 0.0).astype(BF16)
        for kv in range(ATT_KV):
            ks = slice(kv * ATT_HD, (kv + 1) * ATT_HD)
            k = k_ref[0:nk, ks]
            v = jnp.concatenate([v_ref[0:nk, ks], ones_col], axis=1)
            for g in range(group):
                hd = kv * group + g
                sl = slice(hd * ATT_HD, (hd + 1) * ATT_HD)
                s = lax.dot_general(q_ref[:, sl], k, tt, preferred_element_type=F32)
                m = jnp.max(s, axis=-1, keepdims=True)
                e = jnp.exp((s - m).astype(BF16))
                o = jnp.dot(e, v, preferred_element_type=F32)
                o_ref[:, sl] = (o[:, 0:ATT_HD] / o[:, ATT_HD:ATT_HD + 1]).astype(BF16)

    @pl.when(qi == 0)
    def _():
        run(CTX)

    @pl.when(qi > 0)
    def _():
        run(S)


def _attention(p, tq):
    n = p.shape[0]
    b = n // S
    tpb = S // tq
    qw = ATT_HEADS * ATT_HD
    kw = ATT_KV * ATT_HD
    return pl.pallas_call(
        _attn_kernel,
        grid=(b, tpb),
        in_specs=[
            pl.BlockSpec((tq, qw), lambda i, j: (i * tpb + j, 0)),
            pl.BlockSpec((S, kw), lambda i, j: (i, qw // kw)),
            pl.BlockSpec((S, kw), lambda i, j: (i, qw // kw + 1)),
        ],
        out_specs=pl.BlockSpec((tq, qw), lambda i, j: (i * tpb + j, 0)),
        out_shape=jax.ShapeDtypeStruct((n, qw), BF16),
        compiler_params=_cparams(("parallel", "parallel")),
        name="gq_attention",
    )(p, p, p)


def _route_tile(x, mod_ref, w_ref, b_ref, hr_ref, info_ref, cnt_ref, run_ref, tri_ref, tm):
    i = pl.program_id(0)
    j = pl.program_id(1)

    @pl.when((i == 0) & (j == 0))
    def _():
        run_ref[...] = jnp.zeros_like(run_ref)
        r_i = lax.broadcasted_iota(jnp.int32, (tm, tm), 0)
        c_i = lax.broadcasted_iota(jnp.int32, (tm, tm), 1)
        tri_ref[...] = jnp.where(r_i < c_i, 1.0, 0.0).astype(BF16)

    h = _modulated(x, mod_ref, j, tm, 3, 4)
    h_hi = h.astype(BF16)
    h_lo = (h - h_hi.astype(F32)).astype(BF16)
    w = w_ref[...]
    tt = (((1,), (1,)), ((), ()))
    pa = lax.dot_general(w, h_hi, tt, preferred_element_type=F32)
    pb = lax.dot_general(w, h_lo, tt, preferred_element_type=F32)
    nrow = ROUTE_ROWS
    logits = pa[0:nrow, :] + pa[32:32 + nrow, :] + pb[0:nrow, :] + b_ref[...]
    row = lax.broadcasted_iota(jnp.int32, (nrow, tm), 0).astype(F32)
    neg = jnp.float32(-jnp.inf)
    big = jnp.float32(1024.0)

    gl = jnp.where(row < MOE_GROUPS, logits, neg)
    gmax = jnp.max(gl, axis=0, keepdims=True)
    gidx = jnp.min(jnp.where(gl == gmax, row, big), axis=0, keepdims=True)
    gsum = jnp.sum(jnp.exp(gl - gmax), axis=0, keepdims=True)
    p_group = 1.0 / gsum

    base = MOE_GROUPS + MOE_EPG * gidx
    el = jnp.where((row >= base) & (row < base + MOE_EPG), logits, neg)
    emax = jnp.max(el, axis=0, keepdims=True)
    i1 = jnp.min(jnp.where(el == emax, row, big), axis=0, keepdims=True)
    esum = jnp.sum(jnp.exp(el - emax), axis=0, keepdims=True)
    el2 = jnp.where(row == i1, neg, el)
    e2max = jnp.max(el2, axis=0, keepdims=True)
    i2 = jnp.min(jnp.where(el2 == e2max, row, big), axis=0, keepdims=True)
    p1 = 1.0 / esum
    p2 = jnp.exp(e2max - emax) / esum
    w1 = p_group * p1 / (p1 + p2)
    w2 = p_group * p2 / (p1 + p2)

    a1 = i1 - base
    a2 = i2 - base
    first_low = a1 < a2
    lo = jnp.where(first_low, a1, a2)
    hi = jnp.where(first_low, a2, a1)
    w_lo = jnp.where(first_low, w1, w2)
    w_hi = jnp.where(first_low, w2, w1)
    pair = jnp.where(lo == 0.0, hi - 1.0, jnp.where(lo == 1.0, hi + 1.0, 5.0))
    cls = gidx * 6.0 + pair

    onehot = jnp.where(row == cls, 1.0, 0.0)
    before = jnp.dot(onehot.astype(BF16), tri_ref[...], preferred_element_type=F32)
    rank = jnp.sum(onehot * (before + run_ref[...]), axis=0, keepdims=True)
    run_ref[...] = run_ref[...] + jnp.sum(onehot, axis=1, keepdims=True)

    r8 = lax.broadcasted_iota(jnp.int32, (8, tm), 0)
    info = jnp.where(r8 == 0, cls, jnp.where(r8 == 1, rank, jnp.where(r8 == 2, w_lo, jnp.where(r8 == 3, w_hi, 0.0))))
    info_ref[...] = info
    hr_ref[:, 0:D] = h
    hr_ref[:, D:ROUTE_W] = jnp.transpose(jnp.concatenate([info, jnp.zeros((120, tm), F32)], axis=0))
    cnt_ref[...] = run_ref[...]


def _outproj_router_kernel(a_ref, b_ref, x_ref, mod_ref, w_ref, rw_ref, rb_ref,
                           o_ref, hr_ref, info_ref, cnt_ref, run_ref, tri_ref, *, tm):
    j = pl.program_id(1)
    half = D // 2
    acc = (jnp.dot(a_ref[...], w_ref[0:half, :], preferred_element_type=F32)
           + jnp.dot(b_ref[...], w_ref[half:D, :], preferred_element_type=F32))
    x1 = x_ref[...] + _row_select(mod_ref, 2, j, tm) * acc
    o_ref[...] = x1
    _route_tile(x1, mod_ref, rw_ref, rb_ref, hr_ref, info_ref, cnt_ref, run_ref, tri_ref, tm)


def _outproj_router(a, a_col, bm, b_col, x, mod, w, w2, bias, tm):
    n = x.shape[0]
    b = n // S
    tpb = S // tm
    half = D // 2
    return pl.pallas_call(
        functools.partial(_outproj_router_kernel, tm=tm),
        grid=(b, tpb),
        in_specs=[
            pl.BlockSpec((tm, half), lambda i, j: (i * tpb + j, a_col)),
            pl.BlockSpec((tm, half), lambda i, j: (i * tpb + j, b_col)),
            pl.BlockSpec((tm, D), lambda i, j: (i * tpb + j, 0)),
            pl.BlockSpec((None, 2, 6, D), lambda i, j: (i, 0, 0, 0)),
            pl.BlockSpec((D, D), lambda i, j: (0, 0)),
            pl.BlockSpec((128, D), lambda i, j: (0, 0)),
            pl.BlockSpec((ROUTE_ROWS, 1), lambda i, j: (0, 0)),
        ],
        out_specs=[
            pl.BlockSpec((tm, D), lambda i, j: (i * tpb + j, 0)),
            pl.BlockSpec((tm, ROUTE_W), lambda i, j: (i * tpb + j, 0)),
            pl.BlockSpec((None, 8, tm), lambda i, j: (i * tpb + j, 0, 0)),
            pl.BlockSpec((ROUTE_ROWS, 1), lambda i, j: (0, 0)),
        ],
        out_shape=[
            jax.ShapeDtypeStruct((n, D), F32),
            jax.ShapeDtypeStruct((n, ROUTE_W), F32),
            jax.ShapeDtypeStruct((n // tm, 8, tm), F32),
            jax.ShapeDtypeStruct((ROUTE_ROWS, 1), F32),
        ],
        scratch_shapes=[pltpu.VMEM((ROUTE_ROWS, 1), F32), pltpu.VMEM((tm, tm), BF16)],
        compiler_params=_cparams(("arbitrary", "arbitrary")),
        name="outproj_router",
    )(a, bm, x, mod, w, w2, bias)


def _dispatch_kernel(pos_ref, hr_ref, xs_in_ref, xs_ref, sem, *, td):
    del xs_in_ref

    def body(r8, carry):
        for u in range(SUBLANES):
            dst = pos_ref[0, 0, r8 * SUBLANES + u]
            pltpu.make_async_copy(hr_ref.at[r8, pl.ds(u, 1)], xs_ref.at[pl.ds(dst, 1)], sem).start()
        return carry

    lax.fori_loop(0, td // SUBLANES, body, 0)
    pltpu.make_async_copy(xs_ref.at[pl.ds(0, td)], xs_ref.at[pl.ds(0, td)], sem).wait()


def _dispatch(pos, hr, xs_init, td):
    n = hr.shape[0]
    return pl.pallas_call(
        functools.partial(_dispatch_kernel, td=td),
        grid=(n // td,),
        in_specs=[
            pl.BlockSpec((1, 1, td), lambda i: (i, 0, 0), memory_space=pltpu.SMEM),
            pl.BlockSpec((td // SUBLANES, SUBLANES, ROUTE_W), lambda i: (i, 0, 0)),
            pl.BlockSpec(memory_space=pl.ANY),
        ],
        out_specs=pl.BlockSpec(memory_space=pl.ANY),
        out_shape=jax.ShapeDtypeStruct(xs_init.shape, F32),
        scratch_shapes=[pltpu.SemaphoreType.DMA],
        input_output_aliases={2: 0},
        compiler_params=_cparams(("arbitrary",)),
        name="moe_dispatch",
    )(pos.reshape(n // td, 1, td), hr.reshape(n // SUBLANES, SUBLANES, ROUTE_W), xs_init)


def _expert_kernel(be_ref, xs_ref, wga_ref, wua_ref, wda_ref, wgb_ref, wub_ref, wdb_ref, ys_ref,
                   wgu_scr, wd_scr):
    x = xs_ref[:, 0:D].astype(BF16)
    w_lo = xs_ref[:, D + 2:D + 3]
    w_hi = xs_ref[:, D + 3:D + 4]

    @pl.when(be_ref[2, pl.program_id(0)] == 1)
    def _():
        wgu_scr[0] = wga_ref[...].astype(BF16)
        wgu_scr[1] = wua_ref[...].astype(BF16)
        wgu_scr[2] = wgb_ref[...].astype(BF16)
        wgu_scr[3] = wub_ref[...].astype(BF16)
        wd_scr[0] = wda_ref[...].astype(BF16)
        wd_scr[1] = wdb_ref[...].astype(BF16)

    def expert(e):
        g = jnp.dot(x, wgu_scr[2 * e], preferred_element_type=F32)
        u = jnp.dot(x, wgu_scr[2 * e + 1], preferred_element_type=F32)
        hdn = (_silu(g) * u).astype(BF16)
        return jnp.dot(hdn, wd_scr[e], preferred_element_type=F32)

    ys_ref[...] = w_lo * expert(0) + w_hi * expert(1)


def _experts(blk_e, xs, weights, tmb):
    npad = xs.shape[0]
    nblk = npad // tmb
    layer, wg, wu, wd = weights

    def wspec(shape, which):
        return pl.BlockSpec((None, None) + shape, lambda i, be: (layer, be[which, i], 0, 0))

    grid_spec = pltpu.PrefetchScalarGridSpec(
        num_scalar_prefetch=1,
        grid=(nblk,),
        in_specs=[
            pl.BlockSpec((tmb, ROUTE_W), lambda i, be: (i, 0)),
            wspec((D, MOE_HIDDEN), 0), wspec((D, MOE_HIDDEN), 0), wspec((MOE_HIDDEN, D), 0),
            wspec((D, MOE_HIDDEN), 1), wspec((D, MOE_HIDDEN), 1), wspec((MOE_HIDDEN, D), 1),
        ],
        out_specs=pl.BlockSpec((tmb, D), lambda i, be: (i, 0)),
        scratch_shapes=[
            pltpu.VMEM((4, D, MOE_HIDDEN), BF16),
            pltpu.VMEM((2, MOE_HIDDEN, D), BF16),
        ],
    )
    return pl.pallas_call(
        _expert_kernel,
        grid_spec=grid_spec,
        out_shape=jax.ShapeDtypeStruct((npad, D), F32),
        compiler_params=_cparams(("arbitrary",)),
        name="moe_experts",
    )(blk_e, xs, wg, wu, wd, wg, wu, wd)


def _combine_kernel(pos_ref, ys_ref, x_ref, mod_ref, fg_ref, o_ref, buf, sem, *, tc, final):
    j = pl.program_id(1)

    def body(r8, carry):
        for u in range(SUBLANES):
            src = pos_ref[0, 0, r8 * SUBLANES + u]
            pltpu.make_async_copy(ys_ref.at[pl.ds(src, 1)], buf.at[r8, pl.ds(u, 1)], sem).start()
        return carry

    lax.fori_loop(0, tc // SUBLANES, body, 0)
    pltpu.make_async_copy(ys_ref.at[pl.ds(0, tc)], ys_ref.at[pl.ds(0, tc)], sem).wait()
    y = buf[...].reshape(tc, D)
    if final:
        o = x_ref[...] + mod_ref[1, 5:6, :] * y
        ms = jnp.mean(o * o, axis=-1, keepdims=True)
        o = o * lax.rsqrt(ms + EPS) * fg_ref[...]
    else:
        o = x_ref[...] + _row_select(mod_ref, 5, j, tc) * y
    o_ref[...] = o


def _combine(pos, ys, x, mod, fg, tc, final):
    n = x.shape[0]
    b = n // S
    tpb = S // tc
    joff = CTX // tc if final else 0
    tpo = tpb - joff
    return pl.pallas_call(
        functools.partial(_combine_kernel, tc=tc, final=final),
        grid=(b, tpo),
        in_specs=[
            pl.BlockSpec((1, 1, tc), lambda i, j: (i * tpb + joff + j, 0, 0), memory_space=pltpu.SMEM),
            pl.BlockSpec(memory_space=pl.ANY),
            pl.BlockSpec((tc, D), lambda i, j: (i * tpb + joff + j, 0)),
            pl.BlockSpec((None, 2, 6, D), lambda i, j: (i, 0, 0, 0)),
            pl.BlockSpec((1, D), lambda i, j: (0, 0)),
        ],
        out_specs=pl.BlockSpec((tc, D), lambda i, j: (i * tpo + j, 0)),
        out_shape=jax.ShapeDtypeStruct((b * tpo * tc, D), F32),
        scratch_shapes=[pltpu.VMEM((tc // SUBLANES, SUBLANES, D), F32), pltpu.SemaphoreType.DMA],
        compiler_params=_cparams(("arbitrary", "arbitrary")),
        name="moe_combine",
    )(pos.reshape(n // tc, 1, tc), ys, x, mod, fg)


_PAIR_LO = np.array([0, 0, 0, 1, 1, 2], np.int32)
_PAIR_HI = np.array([1, 2, 3, 2, 3, 3], np.int32)


def _even_weights(w_in, dec_w_f, dec_b_f, dec_w_b, dec_b_b):
    hk = GLA_HEADS * GLA_DK
    hv = GLA_HEADS * GLA_DV
    o_q, o_k, o_v, o_r = 0, hk, 2 * hk, 2 * hk + hv
    o_zf = o_r + hv
    o_zb = o_zf + GLA_RANK
    o_ca = o_zb + GLA_RANK
    o_cg = o_ca + CONV_CH
    main = jnp.concatenate([w_in[:, o_q:o_zf], w_in[:, o_ca:o_cg + CONV_CH]], axis=1)
    gates = jnp.concatenate([w_in[:, o_zf:o_ca], jnp.zeros((D, 128 - 2 * GLA_RANK), F32)], axis=1)
    w = jnp.concatenate([main, gates], axis=1).astype(BF16)
    wdec = jnp.zeros((128, 2 * hk), F32)
    wdec = wdec.at[0:GLA_RANK, 0:hk].set(dec_w_f).at[GLA_RANK:2 * GLA_RANK, hk:].set(dec_w_b)
    bdec = jnp.concatenate([dec_b_f, dec_b_b]).reshape(1, 2 * hk)
    return w, wdec.astype(BF16), bdec


def _odd_weights(w_in, q_norm_g, k_norm_g):
    perm = np.concatenate([np.arange(0, ATT_HD, 2), np.arange(1, ATT_HD, 2)])
    nqk = ATT_HEADS + ATT_KV
    cols = np.concatenate([h * ATT_HD + perm for h in range(nqk)]
                          + [np.arange(nqk * ATT_HD, ODD_W)])
    w = w_in[:, cols].astype(BF16)
    gq = q_norm_g[perm] * (ATT_HD ** -0.5)
    gain = jnp.concatenate([jnp.tile(gq, ATT_HEADS), jnp.tile(k_norm_g[perm], ATT_KV)]).reshape(1, -1)
    return w, gain


def _rope_tables():
    t = jnp.arange(SEQ)
    row = (t // GRID_W).astype(F32)
    col = (t % GRID_W).astype(F32)
    ppa = ATT_HD // 4
    inv = ROPE_THETA ** (-jnp.arange(ppa, dtype=F32) / ppa)
    ang = jnp.concatenate([row[:, None] * inv, col[:, None] * inv], axis=-1)
    cos = jnp.cos(ang)
    sin = jnp.sin(ang)
    cos_t = jnp.concatenate([jnp.ones((CTX, ATT_HD), F32), jnp.concatenate([cos, cos], axis=1)], axis=0)
    sin_t = jnp.concatenate([jnp.zeros((CTX, ATT_HD), F32), jnp.concatenate([-sin, sin], axis=1)], axis=0)
    return cos_t, sin_t


def _router_weights(w_group, b_group, w_router, b_router):
    wl = jnp.concatenate([w_group, w_router.reshape(D, MOE_EXPERTS)], axis=1).T
    hi = wl.astype(BF16)
    lo = (wl - hi.astype(F32)).astype(BF16)
    nl = MOE_GROUPS + MOE_EXPERTS
    w2 = jnp.zeros((128, D), BF16).at[0:nl].set(hi).at[32:32 + nl].set(lo)
    bias = jnp.zeros((ROUTE_ROWS, 1), F32).at[0:nl, 0].set(jnp.concatenate([b_group, b_router.reshape(-1)]))
    return w2, bias


def _moe(x, routed, mod, weights, fg, xs_buf, final, tmb, td, tc):
    n = x.shape[0]
    hr, info, cnt = routed
    counts = jnp.round(cnt[0:MOE_CLASSES, 0]).astype(jnp.int32)
    padded = ((counts + tmb - 1) // tmb) * tmb
    ends = jnp.cumsum(padded)
    offs = ends - padded
    cls = jnp.round(info[:, 0, :]).astype(jnp.int32)
    rank = jnp.round(info[:, 1, :]).astype(jnp.int32)
    pos = rank
    for k in range(MOE_CLASSES):
        pos = pos + jnp.where(cls == k, offs[k], 0)
    pos = pos.reshape(n)
    npad = n + MOE_CLASSES * tmb
    nblk = npad // tmb
    starts = jnp.arange(nblk, dtype=jnp.int32) * tmb
    blk_cls = jnp.minimum(jnp.sum((ends[None, :] <= starts[:, None]).astype(jnp.int32), axis=1),
                          MOE_CLASSES - 1)
    grp = blk_cls // 6
    pr = blk_cls % 6
    changed = jnp.concatenate([jnp.ones((1,), jnp.int32),
                               (blk_cls[1:] != blk_cls[:-1]).astype(jnp.int32)])
    blk_e = jnp.stack([grp * MOE_EPG + jnp.asarray(_PAIR_LO)[pr],
                       grp * MOE_EPG + jnp.asarray(_PAIR_HI)[pr], changed], axis=0).astype(jnp.int32)
    xs = _dispatch(pos, hr, xs_buf, td)
    ys = _experts(blk_e, xs, weights, tmb)
    return _combine(pos, ys, x, mod, fg, CTX if final else tc, final), xs


def kernel(x, c, ctx, c_ctx, ada_w, ada_b, even_w_in, even_dec_w_f, even_dec_b_f, even_dec_w_b,
           even_dec_b_b, even_gla_norm_g, even_conv_w, even_conv_b, even_conv_norm_g, even_conv_norm_b,
           even_w_out, odd_w_in, odd_q_norm_g, odd_k_norm_g, odd_w_out, moe_w_group, moe_b_group,
           moe_w_router, moe_b_router, moe_w_gate, moe_w_up, moe_w_down, final_norm_g):
    b = x.shape[0]
    n = b * S
    tp = PROJ_TILE

    cvec = jnp.zeros((40, D), F32).at[0:b].set(c).at[b].set(c_ctx)
    mods = _mods(cvec, ada_w, ada_b)
    mod_lat = mods[:, 0:b].reshape(DEPTH, b, 1, 6, D)
    mod_ctx = jnp.broadcast_to(mods[:, b].reshape(DEPTH, 1, 1, 6, D), (DEPTH, b, 1, 6, D))
    mod_all = jnp.concatenate([mod_ctx, mod_lat], axis=2)

    xc = jnp.concatenate([ctx, x], axis=1).reshape(n, D)
    cos_t, sin_t = _rope_tables()
    fg = final_norm_g.reshape(1, D)
    xs_buf = jnp.zeros((n + MOE_CLASSES * MOE_BLOCK, ROUTE_W), F32)

    for l in range(DEPTH):
        i = l // 2
        mod = mod_all[l]
        w2, bias = _router_weights(moe_w_group[l], moe_b_group[l], moe_w_router[l], moe_b_router[l])
        if l % 2 == 0:
            w, wdec, bdec = _even_weights(even_w_in[i], even_dec_w_f[i], even_dec_b_f[i],
                                          even_dec_w_b[i], even_dec_b_b[i])
            p = _inproj(xc, mod, w, EVEN_W, tp)
            og = _gla(p, wdec, bdec, even_gla_norm_g[i].reshape(1, GLA_DV))
            cw8 = jnp.broadcast_to(even_conv_w[i].reshape(CONV_WIDTH, 1, CONV_CH), (CONV_WIDTH, 8, CONV_CH))
            oc = _conv(p, cw8, even_conv_b[i].reshape(1, -1), even_conv_norm_g[i].reshape(1, -1),
                       even_conv_norm_b[i].reshape(1, -1))
            xc, *routed = _outproj_router(og, 0, oc, 0, xc, mod, even_w_out[i].astype(BF16), w2, bias, tp)
        else:
            w, gain = _odd_weights(odd_w_in[i], odd_q_norm_g[i], odd_k_norm_g[i])
            p = _inproj(xc, mod, w, ODD_W, QK_PROJ_TILE, extra=(gain, cos_t, sin_t))
            oa = _attention(p, CTX)
            xc, *routed = _outproj_router(oa, 0, oa, 1, xc, mod, odd_w_out[i].astype(BF16), w2, bias, tp)
        xc, xs_buf = _moe(xc, routed, mod, (l, moe_w_gate, moe_w_up, moe_w_down), fg, xs_buf,
                          l == DEPTH - 1, MOE_BLOCK, DISPATCH_TILE, COMBINE_TILE)

    return xc.reshape(b, SEQ, D)
```

```python
import functools

import jax
import jax.numpy as jnp
import numpy as np
from jax import lax
from jax.experimental import pallas as pl
from jax.experimental.pallas import tpu as pltpu

F32 = jnp.float32
BF16 = jnp.bfloat16

D = 1024
CTX = 256
SEQ = 2048
S = CTX + SEQ
DEPTH = 4
GRID_W = 64
EPS = 1e-6
GLA_HEADS = 4
GLA_DV = 128
GLA_DK = 64
GLA_RANK = 16
GLA_NORMALIZER = 16.0
CONV_CH = 512
CONV_WIDTH = 31
ATT_HD = 128
ATT_HEADS = 8
ATT_KV = 2
ROPE_THETA = 10000.0
MOE_GROUPS = 4
MOE_EPG = 4
MOE_EXPERTS = 16
MOE_HIDDEN = 512
MOE_CLASSES = MOE_GROUPS * 6

EVEN_W = 2688
ODD_W = 1536
GLA_CHUNK = 128
GLA_UNROLL = 2
CONV_RT = 64
CONV_STRAND_CTX = 36
CONV_STRAND_LAT = 260
CONV_U_CTX = 16
CONV_U_LAT = CONV_U_CTX + 8 * CONV_STRAND_CTX + 32
CONV_U_ROWS = CONV_U_LAT + 8 * CONV_STRAND_LAT + 16
CONV_V_LAT = 8 * CONV_STRAND_CTX
CONV_V_ROWS = CONV_V_LAT + 8 * CONV_STRAND_LAT
ROUTE_W = D + 128
SUBLANES = 8
MOE_BLOCK = 256
ROUTE_ROWS = 24
PROJ_TILE = 768
QK_PROJ_TILE = 256
DISPATCH_TILE = S
COMBINE_TILE = S // 2
VMEM_LIMIT = 48 * 1024 * 1024


def _cparams(sem):
    return pltpu.CompilerParams(dimension_semantics=sem, vmem_limit_bytes=VMEM_LIMIT)


def _sigmoid(v):
    return 0.5 * jnp.tanh(0.5 * v) + 0.5


def _silu(v):
    return v * _sigmoid(v)


def _mods_kernel(c_ref, w_ref, b_ref, o_ref):
    a = _silu(c_ref[...])
    o_ref[...] = jnp.dot(a, w_ref[...], preferred_element_type=F32,
                         precision=lax.Precision.HIGHEST) + b_ref[...]


def _mods(cvec, ada_w, ada_b):
    nl, _, n6 = ada_w.shape
    rows = cvec.shape[0]
    nb = n6 // D
    return pl.pallas_call(
        _mods_kernel,
        grid=(nl, nb),
        in_specs=[
            pl.BlockSpec((rows, D), lambda l, j: (0, 0)),
            pl.BlockSpec((None, D, D), lambda l, j: (l, 0, j)),
            pl.BlockSpec((None, 1, D), lambda l, j: (l, 0, j)),
        ],
        out_specs=pl.BlockSpec((None, rows, D), lambda l, j: (l, 0, j)),
        out_shape=jax.ShapeDtypeStruct((nl, rows, n6), F32),
        compiler_params=_cparams(("parallel", "parallel")),
        name="adaln_mods",
    )(cvec, ada_w, ada_b.reshape(nl, 1, n6))


def _row_select(mod_ref, idx, j, tm):
    srow = j * tm + lax.broadcasted_iota(jnp.int32, (tm, 1), 0)
    return jnp.where(srow < CTX, mod_ref[0, idx:idx + 1, :], mod_ref[1, idx:idx + 1, :])


def _modulated(x, mod_ref, j, tm, shift_idx, scale_idx):
    ms = jnp.mean(x * x, axis=-1, keepdims=True)
    xn = x * lax.rsqrt(ms + EPS)
    return xn * (1.0 + _row_select(mod_ref, scale_idx, j, tm)) + _row_select(mod_ref, shift_idx, j, tm)


def _inproj_even_kernel(x_ref, mod_ref, w_ref, o_ref, *, tm):
    j = pl.program_id(1)
    h = _modulated(x_ref[...], mod_ref, j, tm, 0, 1)
    o_ref[...] = jnp.dot(h.astype(BF16), w_ref[...], preferred_element_type=F32).astype(BF16)


def _inproj_odd_kernel(x_ref, mod_ref, w_ref, gain_ref, cos_ref, sin_ref, o_ref, *, tm):
    j = pl.program_id(1)
    h = _modulated(x_ref[...], mod_ref, j, tm, 0, 1)
    p = jnp.dot(h.astype(BF16), w_ref[...], preferred_element_type=F32)
    cos = cos_ref[...]
    sin = sin_ref[...]
    nqk = ATT_HEADS + ATT_KV
    for hd in range(nqk):
        sl = slice(hd * ATT_HD, (hd + 1) * ATT_HD)
        ph = p[:, sl]
        ms = jnp.mean(ph * ph, axis=-1, keepdims=True)
        ph = ph * lax.rsqrt(ms + EPS) * gain_ref[:, sl]
        ph = ph * cos + pltpu.roll(ph, ATT_HD // 2, axis=1) * sin
        o_ref[:, sl] = ph.astype(BF16)
    o_ref[:, nqk * ATT_HD:] = p[:, nqk * ATT_HD:].astype(BF16)


def _inproj(x, mod, w, width, tm, extra=None):
    n = x.shape[0]
    b = n // S
    tpb = S // tm
    in_specs = [
        pl.BlockSpec((tm, D), lambda i, j: (i * tpb + j, 0)),
        pl.BlockSpec((None, 2, 6, D), lambda i, j: (i, 0, 0, 0)),
        pl.BlockSpec((D, width), lambda i, j: (0, 0)),
    ]
    args = [x, mod, w]
    if extra is None:
        kern = functools.partial(_inproj_even_kernel, tm=tm)
        name = "inproj_even"
    else:
        gain, cos, sin = extra
        in_specs += [
            pl.BlockSpec((1, (ATT_HEADS + ATT_KV) * ATT_HD), lambda i, j: (0, 0)),
            pl.BlockSpec((tm, ATT_HD), lambda i, j: (j, 0)),
            pl.BlockSpec((tm, ATT_HD), lambda i, j: (j, 0)),
        ]
        args += [gain, cos, sin]
        kern = functools.partial(_inproj_odd_kernel, tm=tm)
        name = "inproj_odd"
    return pl.pallas_call(
        kern,
        grid=(b, tpb),
        in_specs=in_specs,
        out_specs=pl.BlockSpec((tm, width), lambda i, j: (i * tpb + j, 0)),
        out_shape=jax.ShapeDtypeStruct((n, width), BF16),
        compiler_params=_cparams(("parallel", "parallel")),
        name=name,
    )(*args)


def _log_sigmoid(v):
    return jnp.minimum(v, 0.0) - jnp.log(1.0 + jnp.exp(-jnp.abs(v)))


def _split3(v):
    hi = v.astype(BF16)
    r1 = v - hi.astype(F32)
    mid = r1.astype(BF16)
    lo = (r1 - mid.astype(F32)).astype(BF16)
    return hi, mid, lo


def _gla_kernel(qk_ref, v_ref, z_ref, r_ref, wdec_ref, bdec_ref, gn_ref, o_ref, of_ref, ob_ref, st_ref):
    c = GLA_CHUNK
    nch = S // c
    ncc = CTX // c
    hk = GLA_HEADS * GLA_DK
    rows = lax.broadcasted_iota(jnp.int32, (c, c), 0)
    cols = lax.broadcasted_iota(jnp.int32, (c, c), 1)
    tt = (((1,), (1,)), ((), ()))
    ta = (((0,), (0,)), ((), ()))

    def chunk(ci, fwd):
        r0 = pl.multiple_of(ci * c, c)
        qk = qk_ref[pl.ds(r0, c), :]
        v = v_ref[pl.ds(r0, c), :]
        z = z_ref[pl.ds(r0, c), :]
        g0 = 0 if fwd else hk
        gl = jnp.dot(z, wdec_ref[:, g0:g0 + hk], preferred_element_type=F32) + bdec_ref[:, g0:g0 + hk]
        la = _log_sigmoid(gl) * (1.0 / GLA_NORMALIZER)
        keep = (rows >= cols) if fwd else (rows <= cols)
        tri = jnp.where(keep, 1.0, 0.0).astype(BF16)
        hi, mid, lo = _split3(la)
        cum = (jnp.dot(tri, hi, preferred_element_type=F32)
               + jnp.dot(tri, mid, preferred_element_type=F32)
               + jnp.dot(tri, lo, preferred_element_type=F32))
        total = cum[c - 1:c, :] if fwd else cum[0:1, :]
        mref = cum[c // 2:c // 2 + 1, :]
        q = qk[:, 0:hk].astype(F32) * (GLA_DK ** -0.5)
        k = qk[:, hk:2 * hk].astype(F32)
        q_in = (q * jnp.exp(cum - mref)).astype(BF16)
        k_in = (k * jnp.exp(mref - cum)).astype(BF16)
        q_st = (q * jnp.exp(cum)).astype(BF16)
        k_end = (k * jnp.exp(total - cum)).astype(BF16)
        dec = jnp.exp(total)
        for h in range(GLA_HEADS):
            ks = slice(h * GLA_DK, (h + 1) * GLA_DK)
            vs = slice(h * GLA_DV, (h + 1) * GLA_DV)
            att = lax.dot_general(q_in[:, ks], k_in[:, ks], tt, preferred_element_type=F32)
            att = jnp.where(keep, att, 0.0).astype(BF16)
            sref = st_ref.at[0 if fwd else 1]
            st = sref[h]
            o = (jnp.dot(att, v[:, vs], preferred_element_type=F32)
                 + lax.dot_general(q_st[:, ks], st.astype(BF16), tt, preferred_element_type=F32))
            u = lax.dot_general(v[:, vs], k_end[:, ks], ta, preferred_element_type=F32)
            sref[h] = st * dec[:, ks] + u
            dst = of_ref if fwd else ob_ref
            dst[pl.ds(r0, c), vs] = o

    st_ref[...] = jnp.zeros_like(st_ref)

    def scan_body(tu, carry):
        for u in range(GLA_UNROLL):
            t = tu * GLA_UNROLL + u
            chunk(t, True)
            chunk(jnp.where(t < ncc, ncc - 1 - t, nch - 1 - (t - ncc)), False)
        return carry

    lax.fori_loop(0, nch // GLA_UNROLL, scan_body, 0)

    def out_body(t, carry):
        r0 = pl.multiple_of(t * c, c)
        for h in range(GLA_HEADS):
            vs = slice(h * GLA_DV, (h + 1) * GLA_DV)
            o = of_ref[pl.ds(r0, c), vs] + ob_ref[pl.ds(r0, c), vs]
            ms = jnp.mean(o * o, axis=-1, keepdims=True)
            o = o * lax.rsqrt(ms + EPS) * gn_ref[...]
            o_ref[pl.ds(r0, c), vs] = (o * _silu(r_ref[pl.ds(r0, c), vs].astype(F32))).astype(BF16)
        return carry

    lax.fori_loop(0, nch, out_body, 0)


def _gla(p, wdec, bdec, gn):
    n = p.shape[0]
    b = n // S
    hv = GLA_HEADS * GLA_DV
    return pl.pallas_call(
        _gla_kernel,
        grid=(b,),
        in_specs=[
            pl.BlockSpec((S, 512), lambda i: (i, 0)),
            pl.BlockSpec((S, 512), lambda i: (i, 1)),
            pl.BlockSpec((S, 128), lambda i: (i, 20)),
            pl.BlockSpec((S, 512), lambda i: (i, 2)),
            pl.BlockSpec((128, 512), lambda i: (0, 0)),
            pl.BlockSpec((1, 512), lambda i: (0, 0)),
            pl.BlockSpec((1, GLA_DV), lambda i: (0, 0)),
        ],
        out_specs=pl.BlockSpec((S, hv), lambda i: (i, 0)),
        out_shape=jax.ShapeDtypeStruct((n, hv), BF16),
        scratch_shapes=[
            pltpu.VMEM((S, hv), F32),
            pltpu.VMEM((S, hv), F32),
            pltpu.VMEM((2, GLA_HEADS, GLA_DV, GLA_DK), F32),
        ],
        compiler_params=_cparams(("parallel",)),
        name="gla",
    )(p, p, p, p, wdec, bdec, gn)


def _conv_kernel(ca_ref, cg_ref, w_ref, cb_ref, lg_ref, lb_ref, o_ref, u_ref, v_ref):
    half = CONV_WIDTH // 2
    nslab = CONV_CH // 128
    for sl in range(nslab):
        ls = slice(sl * 128, (sl + 1) * 128)
        u_ref[sl, 0:CONV_U_CTX, :] = jnp.zeros((CONV_U_CTX, 128), F32)
        u_ref[sl, CONV_U_CTX + CTX:CONV_U_LAT, :] = jnp.zeros((CONV_U_LAT - CONV_U_CTX - CTX, 128), F32)
        u_ref[sl, CONV_U_LAT + SEQ:CONV_U_ROWS, :] = jnp.zeros((CONV_U_ROWS - CONV_U_LAT - SEQ, 128), F32)
        u_ref[sl, CONV_U_CTX:CONV_U_CTX + CTX, :] = (
            ca_ref[0:CTX, ls].astype(F32) * _sigmoid(cg_ref[0:CTX, ls].astype(F32)))
        u_ref[sl, CONV_U_LAT:CONV_U_LAT + SEQ, :] = (
            ca_ref[CTX:S, ls].astype(F32) * _sigmoid(cg_ref[CTX:S, ls].astype(F32)))

    def segment(ubase, vbase, strand, rb):
        for sl in range(nslab):
            ls = slice(sl * 128, (sl + 1) * 128)

            def body(blk, carry):
                l0 = blk * rb
                row0 = ubase + l0 - half
                win = [u_ref[sl, pl.ds(row0 + i, 8, stride=strand), :] for i in range(rb + 2 * half)]
                accs = [None] * rb
                for j in range(CONV_WIDTH):
                    wj = w_ref[j, :, ls]
                    for i in range(rb):
                        term = win[i + j] * wj
                        accs[i] = term if j == 0 else accs[i] + term
                for i in range(rb):
                    v_ref[sl, pl.ds(vbase + l0 + i, 8, stride=strand), :] = accs[i]
                return carry

            lax.fori_loop(0, strand // rb, body, 0)

    segment(CONV_U_CTX, 0, CONV_STRAND_CTX, 12)
    segment(CONV_U_LAT, CONV_V_LAT, CONV_STRAND_LAT, 10)

    rt = CONV_RT
    ntile_ctx = CTX // rt

    def norm_body(t, carry):
        r0 = pl.multiple_of(t * rt, rt)
        vrow = pl.multiple_of(r0 + jnp.where(t >= ntile_ctx, CONV_V_LAT - CTX, 0), 8)
        ys = [v_ref[sl, pl.ds(vrow, rt), :] + cb_ref[:, sl * 128:(sl + 1) * 128] for sl in range(nslab)]
        mu = jnp.sum(sum(ys), axis=-1, keepdims=True) * (1.0 / CONV_CH)
        dvs = [y - mu for y in ys]
        var = jnp.sum(sum(d * d for d in dvs), axis=-1, keepdims=True) * (1.0 / CONV_CH)
        inv = lax.rsqrt(var + EPS)
        for sl in range(nslab):
            ls = slice(sl * 128, (sl + 1) * 128)
            y = dvs[sl] * inv * lg_ref[:, ls] + lb_ref[:, ls]
            o_ref[pl.ds(r0, rt), ls] = _silu(y).astype(BF16)
        return carry

    lax.fori_loop(0, S // rt, norm_body, 0)


def _conv(p, w8, cb, lg, lb):
    n = p.shape[0]
    b = n // S
    vec = pl.BlockSpec((1, CONV_CH), lambda i: (0, 0))
    nslab = CONV_CH // 128
    return pl.pallas_call(
        _conv_kernel,
        grid=(b,),
        in_specs=[
            pl.BlockSpec((S, CONV_CH), lambda i: (i, 3)),
            pl.BlockSpec((S, CONV_CH), lambda i: (i, 4)),
            pl.BlockSpec((CONV_WIDTH, 8, CONV_CH), lambda i: (0, 0, 0)),
            vec, vec, vec,
        ],
        out_specs=pl.BlockSpec((S, CONV_CH), lambda i: (i, 0)),
        out_shape=jax.ShapeDtypeStruct((n, CONV_CH), BF16),
        scratch_shapes=[pltpu.VMEM((nslab, CONV_U_ROWS, 128), F32),
                        pltpu.VMEM((nslab, CONV_V_ROWS, 128), F32)],
        compiler_params=_cparams(("parallel",)),
        name="conv_module",
    )(p, p, w8, cb, lg, lb)


def _attn_kernel(q_ref, k_ref, v_ref, o_ref):
    qi = pl.program_id(1)
    tt = (((1,), (1,)), ((), ()))
    group = ATT_HEADS // ATT_KV

    def run(nk):
        lane = lax.broadcasted_iota(jnp.int32, (nk, ATT_HD), 1)
        ones_col = jnp.where(lane == 0, 1.0, 0.0).astype(BF16)
        for kv in range(ATT_KV):
            ks = slice(kv * ATT_HD, (kv + 1) * ATT_HD)
            k = k_ref[0:nk, ks]
            v = jnp.concatenate([v_ref[0:nk, ks], ones_col], axis=1)
            for g in range(group):
                hd = kv * group + g
                sl = slice(hd * ATT_HD, (hd + 1) * ATT_HD)
                s = lax.dot_general(q_ref[:, sl], k, tt, preferred_element_type=F32)
                m = jnp.max(s, axis=-1, keepdims=True)
                e = jnp.exp((s - m).astype(BF16))
                o = jnp.dot(e, v, preferred_element_type=F32)
                o_ref[:, sl] = (o[:, 0:ATT_HD] / o[:, ATT_HD:ATT_HD + 1]).astype(BF16)

    @pl.when(qi == 0)
    def _():
        run(CTX)

    @pl.when(qi > 0)
    def _():
        run(S)


def _attention(p, tq):
    n = p.shape[0]
    b = n // S
    tpb = S // tq
    qw = ATT_HEADS * ATT_HD
    kw = ATT_KV * ATT_HD
    return pl.pallas_call(
        _attn_kernel,
        grid=(b, tpb),
        in_specs=[
            pl.BlockSpec((tq, qw), lambda i, j: (i * tpb + j, 0)),
            pl.BlockSpec((S, kw), lambda i, j: (i, qw // kw)),
            pl.BlockSpec((S, kw), lambda i, j: (i, qw // kw + 1)),
        ],
        out_specs=pl.BlockSpec((tq, qw), lambda i, j: (i * tpb + j, 0)),
        out_shape=jax.ShapeDtypeStruct((n, qw), BF16),
        compiler_params=_cparams(("parallel", "parallel")),
        name="gq_attention",
    )(p, p, p)


def _route_tile(x, mod_ref, w_ref, b_ref, hr_ref, info_ref, cnt_ref, run_ref, tri_ref, tm):
    i = pl.program_id(0)
    j = pl.program_id(1)

    @pl.when((i == 0) & (j == 0))
    def _():
        run_ref[...] = jnp.zeros_like(run_ref)
        r_i = lax.broadcasted_iota(jnp.int32, (tm, tm), 0)
        c_i = lax.broadcasted_iota(jnp.int32, (tm, tm), 1)
        tri_ref[...] = jnp.where(r_i < c_i, 1.0, 0.0).astype(BF16)

    h = _modulated(x, mod_ref, j, tm, 3, 4)
    h_hi = h.astype(BF16)
    h_lo = (h - h_hi.astype(F32)).astype(BF16)
    w = w_ref[...]
    tt = (((1,), (1,)), ((), ()))
    pa = lax.dot_general(w, h_hi, tt, preferred_element_type=F32)
    pb = lax.dot_general(w, h_lo, tt, preferred_element_type=F32)
    nrow = ROUTE_ROWS
    logits = pa[0:nrow, :] + pa[32:32 + nrow, :] + pb[0:nrow, :] + b_ref[...]
    row = lax.broadcasted_iota(jnp.int32, (nrow, tm), 0).astype(F32)
    neg = jnp.float32(-jnp.inf)
    big = jnp.float32(1024.0)

    gl = jnp.where(row < MOE_GROUPS, logits, neg)
    gmax = jnp.max(gl, axis=0, keepdims=True)
    gidx = jnp.min(jnp.where(gl == gmax, row, big), axis=0, keepdims=True)
    gsum = jnp.sum(jnp.exp(gl - gmax), axis=0, keepdims=True)
    p_group = 1.0 / gsum

    base = MOE_GROUPS + MOE_EPG * gidx
    el = jnp.where((row >= base) & (row < base + MOE_EPG), logits, neg)
    emax = jnp.max(el, axis=0, keepdims=True)
    i1 = jnp.min(jnp.where(el == emax, row, big), axis=0, keepdims=True)
    esum = jnp.sum(jnp.exp(el - emax), axis=0, keepdims=True)
    el2 = jnp.where(row == i1, neg, el)
    e2max = jnp.max(el2, axis=0, keepdims=True)
    i2 = jnp.min(jnp.where(el2 == e2max, row, big), axis=0, keepdims=True)
    p1 = 1.0 / esum
    p2 = jnp.exp(e2max - emax) / esum
    w1 = p_group * p1 / (p1 + p2)
    w2 = p_group * p2 / (p1 + p2)

    a1 = i1 - base
    a2 = i2 - base
    first_low = a1 < a2
    lo = jnp.where(first_low, a1, a2)
    hi = jnp.where(first_low, a2, a1)
    w_lo = jnp.where(first_low, w1, w2)
    w_hi = jnp.where(first_low, w2, w1)
    pair = jnp.where(lo == 0.0, hi - 1.0, jnp.where(lo == 1.0, hi + 1.0, 5.0))
    cls = gidx * 6.0 + pair

    onehot = jnp.where(row == cls, 1.0, 0.0)
    before = jnp.dot(onehot.astype(BF16), tri_ref[...], preferred_element_type=F32)
    rank = jnp.sum(onehot * (before + run_ref[...]), axis=0, keepdims=True)
    run_ref[...] = run_ref[...] + jnp.sum(onehot, axis=1, keepdims=True)

    r8 = lax.broadcasted_iota(jnp.int32, (8, tm), 0)
    info = jnp.where(r8 == 0, cls, jnp.where(r8 == 1, rank, jnp.where(r8 == 2, w_lo, jnp.where(r8 == 3, w_hi, 0.0))))
    info_ref[...] = info
    hr_ref[:, 0:D] = h
    hr_ref[:, D:ROUTE_W] = jnp.transpose(jnp.concatenate([info, jnp.zeros((120, tm), F32)], axis=0))
    cnt_ref[...] = run_ref[...]


def _outproj_router_kernel(a_ref, b_ref, x_ref, mod_ref, w_ref, rw_ref, rb_ref,
                           o_ref, hr_ref, info_ref, cnt_ref, run_ref, tri_ref, *, tm):
    j = pl.program_id(1)
    half = D // 2
    acc = (jnp.dot(a_ref[...], w_ref[0:half, :], preferred_element_type=F32)
           + jnp.dot(b_ref[...], w_ref[half:D, :], preferred_element_type=F32))
    x1 = x_ref[...] + _row_select(mod_ref, 2, j, tm) * acc
    o_ref[...] = x1
    _route_tile(x1, mod_ref, rw_ref, rb_ref, hr_ref, info_ref, cnt_ref, run_ref, tri_ref, tm)


def _outproj_router(a, a_col, bm, b_col, x, mod, w, w2, bias, tm):
    n = x.shape[0]
    b = n // S
    tpb = S // tm
    half = D // 2
    return pl.pallas_call(
        functools.partial(_outproj_router_kernel, tm=tm),
        grid=(b, tpb),
        in_specs=[
            pl.BlockSpec((tm, half), lambda i, j: (i * tpb + j, a_col)),
            pl.BlockSpec((tm, half), lambda i, j: (i * tpb + j, b_col)),
            pl.BlockSpec((tm, D), lambda i, j: (i * tpb + j, 0)),
            pl.BlockSpec((None, 2, 6, D), lambda i, j: (i, 0, 0, 0)),
            pl.BlockSpec((D, D), lambda i, j: (0, 0)),
            pl.BlockSpec((128, D), lambda i, j: (0, 0)),
            pl.BlockSpec((ROUTE_ROWS, 1), lambda i, j: (0, 0)),
        ],
        out_specs=[
            pl.BlockSpec((tm, D), lambda i, j: (i * tpb + j, 0)),
            pl.BlockSpec((tm, ROUTE_W), lambda i, j: (i * tpb + j, 0)),
            pl.BlockSpec((None, 8, tm), lambda i, j: (i * tpb + j, 0, 0)),
            pl.BlockSpec((ROUTE_ROWS, 1), lambda i, j: (0, 0)),
        ],
        out_shape=[
            jax.ShapeDtypeStruct((n, D), F32),
            jax.ShapeDtypeStruct((n, ROUTE_W), F32),
            jax.ShapeDtypeStruct((n // tm, 8, tm), F32),
            jax.ShapeDtypeStruct((ROUTE_ROWS, 1), F32),
        ],
        scratch_shapes=[pltpu.VMEM((ROUTE_ROWS, 1), F32), pltpu.VMEM((tm, tm), BF16)],
        compiler_params=_cparams(("arbitrary", "arbitrary")),
        name="outproj_router",
    )(a, bm, x, mod, w, w2, bias)


def _dispatch_kernel(pos_ref, hr_ref, xs_in_ref, xs_ref, sem, *, td):
    del xs_in_ref

    def body(r8, carry):
        for u in range(SUBLANES):
            dst = pos_ref[0, 0, r8 * SUBLANES + u]
            pltpu.make_async_copy(hr_ref.at[r8, pl.ds(u, 1)], xs_ref.at[pl.ds(dst, 1)], sem).start(priority=u % 2)
        return carry

    lax.fori_loop(0, td // SUBLANES, body, 0)
    pltpu.make_async_copy(xs_ref.at[pl.ds(0, td)], xs_ref.at[pl.ds(0, td)], sem).wait()


def _dispatch(pos, hr, xs_init, td):
    n = hr.shape[0]
    return pl.pallas_call(
        functools.partial(_dispatch_kernel, td=td),
        grid=(n // td,),
        in_specs=[
            pl.BlockSpec((1, 1, td), lambda i: (i, 0, 0), memory_space=pltpu.SMEM),
            pl.BlockSpec((td // SUBLANES, SUBLANES, ROUTE_W), lambda i: (i, 0, 0)),
            pl.BlockSpec(memory_space=pl.ANY),
        ],
        out_specs=pl.BlockSpec(memory_space=pl.ANY),
        out_shape=jax.ShapeDtypeStruct(xs_init.shape, F32),
        scratch_shapes=[pltpu.SemaphoreType.DMA],
        input_output_aliases={2: 0},
        compiler_params=_cparams(("arbitrary",)),
        name="moe_dispatch",
    )(pos.reshape(n // td, 1, td), hr.reshape(n // SUBLANES, SUBLANES, ROUTE_W), xs_init)


def _expert_kernel(be_ref, xs_ref, wga_ref, wua_ref, wda_ref, wgb_ref, wub_ref, wdb_ref, ys_ref,
                   wgu_scr, wd_scr):
    x = xs_ref[:, 0:D].astype(BF16)
    w_lo = xs_ref[:, D + 2:D + 3]
    w_hi = xs_ref[:, D + 3:D + 4]

    @pl.when(be_ref[2, pl.program_id(0)] == 1)
    def _():
        wgu_scr[0] = wga_ref[...].astype(BF16)
        wgu_scr[1] = wua_ref[...].astype(BF16)
        wgu_scr[2] = wgb_ref[...].astype(BF16)
        wgu_scr[3] = wub_ref[...].astype(BF16)
        wd_scr[0] = wda_ref[...].astype(BF16)
        wd_scr[1] = wdb_ref[...].astype(BF16)

    def expert(e):
        g = jnp.dot(x, wgu_scr[2 * e], preferred_element_type=F32)
        u = jnp.dot(x, wgu_scr[2 * e + 1], preferred_element_type=F32)
        hdn = (_silu(g) * u).astype(BF16)
        return jnp.dot(hdn, wd_scr[e], preferred_element_type=F32)

    ys_ref[...] = w_lo * expert(0) + w_hi * expert(1)


def _experts(blk_e, xs, weights, tmb):
    npad = xs.shape[0]
    nblk = npad // tmb
    layer, wg, wu, wd = weights

    def wspec(shape, which):
        return pl.BlockSpec((None, None) + shape, lambda i, be: (layer, be[which, i], 0, 0))

    grid_spec = pltpu.PrefetchScalarGridSpec(
        num_scalar_prefetch=1,
        grid=(nblk,),
        in_specs=[
            pl.BlockSpec((tmb, ROUTE_W), lambda i, be: (i, 0)),
            wspec((D, MOE_HIDDEN), 0), wspec((D, MOE_HIDDEN), 0), wspec((MOE_HIDDEN, D), 0),
            wspec((D, MOE_HIDDEN), 1), wspec((D, MOE_HIDDEN), 1), wspec((MOE_HIDDEN, D), 1),
        ],
        out_specs=pl.BlockSpec((tmb, D), lambda i, be: (i, 0)),
        scratch_shapes=[
            pltpu.VMEM((4, D, MOE_HIDDEN), BF16),
            pltpu.VMEM((2, MOE_HIDDEN, D), BF16),
        ],
    )
    return pl.pallas_call(
        _expert_kernel,
        grid_spec=grid_spec,
        out_shape=jax.ShapeDtypeStruct((npad, D), F32),
        compiler_params=_cparams(("arbitrary",)),
        name="moe_experts",
    )(blk_e, xs, wg, wu, wd, wg, wu, wd)


def _combine_kernel(pos_ref, ys_ref, x_ref, mod_ref, fg_ref, o_ref, buf, sem, *, tc, final):
    j = pl.program_id(1)

    def body(r8, carry):
        for u in range(SUBLANES):
            src = pos_ref[0, 0, r8 * SUBLANES + u]
            pltpu.make_async_copy(ys_ref.at[pl.ds(src, 1)], buf.at[r8, pl.ds(u, 1)], sem).start(priority=u % 2)
        return carry

    lax.fori_loop(0, tc // SUBLANES, body, 0)
    pltpu.make_async_copy(ys_ref.at[pl.ds(0, tc)], ys_ref.at[pl.ds(0, tc)], sem).wait()
    y = buf[...].reshape(tc, D)
    if final:
        o = x_ref[...] + mod_ref[1, 5:6, :] * y
        ms = jnp.mean(o * o, axis=-1, keepdims=True)
        o = o * lax.rsqrt(ms + EPS) * fg_ref[...]
    else:
        o = x_ref[...] + _row_select(mod_ref, 5, j, tc) * y
    o_ref[...] = o


def _combine(pos, ys, x, mod, fg, tc, final):
    n = x.shape[0]
    b = n // S
    tpb = S // tc
    joff = CTX // tc if final else 0
    tpo = tpb - joff
    return pl.pallas_call(
        functools.partial(_combine_kernel, tc=tc, final=final),
        grid=(b, tpo),
        in_specs=[
            pl.BlockSpec((1, 1, tc), lambda i, j: (i * tpb + joff + j, 0, 0), memory_space=pltpu.SMEM),
            pl.BlockSpec(memory_space=pl.ANY),
            pl.BlockSpec((tc, D), lambda i, j: (i * tpb + joff + j, 0)),
            pl.BlockSpec((None, 2, 6, D), lambda i, j: (i, 0, 0, 0)),
            pl.BlockSpec((1, D), lambda i, j: (0, 0)),
        ],
        out_specs=pl.BlockSpec((tc, D), lambda i, j: (i * tpo + j, 0)),
        out_shape=jax.ShapeDtypeStruct((b * tpo * tc, D), F32),
        scratch_shapes=[pltpu.VMEM((tc // SUBLANES, SUBLANES, D), F32), pltpu.SemaphoreType.DMA],
        compiler_params=_cparams(("arbitrary", "arbitrary")),
        name="moe_combine",
    )(pos.reshape(n // tc, 1, tc), ys, x, mod, fg)


_PAIR_LO = np.array([0, 0, 0, 1, 1, 2], np.int32)
_PAIR_HI = np.array([1, 2, 3, 2, 3, 3], np.int32)


def _even_weights(w_in, dec_w_f, dec_b_f, dec_w_b, dec_b_b):
    hk = GLA_HEADS * GLA_DK
    hv = GLA_HEADS * GLA_DV
    o_q, o_k, o_v, o_r = 0, hk, 2 * hk, 2 * hk + hv
    o_zf = o_r + hv
    o_zb = o_zf + GLA_RANK
    o_ca = o_zb + GLA_RANK
    o_cg = o_ca + CONV_CH
    main = jnp.concatenate([w_in[:, o_q:o_zf], w_in[:, o_ca:o_cg + CONV_CH]], axis=1)
    gates = jnp.concatenate([w_in[:, o_zf:o_ca], jnp.zeros((D, 128 - 2 * GLA_RANK), F32)], axis=1)
    w = jnp.concatenate([main, gates], axis=1).astype(BF16)
    wdec = jnp.zeros((128, 2 * hk), F32)
    wdec = wdec.at[0:GLA_RANK, 0:hk].set(dec_w_f).at[GLA_RANK:2 * GLA_RANK, hk:].set(dec_w_b)
    bdec = jnp.concatenate([dec_b_f, dec_b_b]).reshape(1, 2 * hk)
    return w, wdec.astype(BF16), bdec


def _odd_weights(w_in, q_norm_g, k_norm_g):
    perm = np.concatenate([np.arange(0, ATT_HD, 2), np.arange(1, ATT_HD, 2)])
    nqk = ATT_HEADS + ATT_KV
    cols = np.concatenate([h * ATT_HD + perm for h in range(nqk)]
                          + [np.arange(nqk * ATT_HD, ODD_W)])
    w = w_in[:, cols].astype(BF16)
    gq = q_norm_g[perm] * (ATT_HD ** -0.5)
    gain = jnp.concatenate([jnp.tile(gq, ATT_HEADS), jnp.tile(k_norm_g[perm], ATT_KV)]).reshape(1, -1)
    return w, gain


def _rope_tables():
    t = jnp.arange(SEQ)
    row = (t // GRID_W).astype(F32)
    col = (t % GRID_W).astype(F32)
    ppa = ATT_HD // 4
    inv = ROPE_THETA ** (-jnp.arange(ppa, dtype=F32) / ppa)
    ang = jnp.concatenate([row[:, None] * inv, col[:, None] * inv], axis=-1)
    cos = jnp.cos(ang)
    sin = jnp.sin(ang)
    cos_t = jnp.concatenate([jnp.ones((CTX, ATT_HD), F32), jnp.concatenate([cos, cos], axis=1)], axis=0)
    sin_t = jnp.concatenate([jnp.zeros((CTX, ATT_HD), F32), jnp.concatenate([-sin, sin], axis=1)], axis=0)
    return cos_t, sin_t


def _router_weights(w_group, b_group, w_router, b_router):
    wl = jnp.concatenate([w_group, w_router.reshape(D, MOE_EXPERTS)], axis=1).T
    hi = wl.astype(BF16)
    lo = (wl - hi.astype(F32)).astype(BF16)
    nl = MOE_GROUPS + MOE_EXPERTS
    w2 = jnp.zeros((128, D), BF16).at[0:nl].set(hi).at[32:32 + nl].set(lo)
    bias = jnp.zeros((ROUTE_ROWS, 1), F32).at[0:nl, 0].set(jnp.concatenate([b_group, b_router.reshape(-1)]))
    return w2, bias


def _moe(x, routed, mod, weights, fg, xs_buf, final, tmb, td, tc):
    n = x.shape[0]
    hr, info, cnt = routed
    counts = jnp.round(cnt[0:MOE_CLASSES, 0]).astype(jnp.int32)
    padded = ((counts + tmb - 1) // tmb) * tmb
    ends = jnp.cumsum(padded)
    offs = ends - padded
    cls = jnp.round(info[:, 0, :]).astype(jnp.int32)
    rank = jnp.round(info[:, 1, :]).astype(jnp.int32)
    pos = rank
    for k in range(MOE_CLASSES):
        pos = pos + jnp.where(cls == k, offs[k], 0)
    pos = pos.reshape(n)
    npad = n + MOE_CLASSES * tmb
    nblk = npad // tmb
    starts = jnp.arange(nblk, dtype=jnp.int32) * tmb
    blk_cls = jnp.minimum(jnp.sum((ends[None, :] <= starts[:, None]).astype(jnp.int32), axis=1),
                          MOE_CLASSES - 1)
    grp = blk_cls // 6
    pr = blk_cls % 6
    changed = jnp.concatenate([jnp.ones((1,), jnp.int32),
                               (blk_cls[1:] != blk_cls[:-1]).astype(jnp.int32)])
    blk_e = jnp.stack([grp * MOE_EPG + jnp.asarray(_PAIR_LO)[pr],
                       grp * MOE_EPG + jnp.asarray(_PAIR_HI)[pr], changed], axis=0).astype(jnp.int32)
    xs = _dispatch(pos, hr, xs_buf, td)
    ys = _experts(blk_e, xs, weights, tmb)
    return _combine(pos, ys, x, mod, fg, CTX if final else tc, final), xs


def kernel(x, c, ctx, c_ctx, ada_w, ada_b, even_w_in, even_dec_w_f, even_dec_b_f, even_dec_w_b,
           even_dec_b_b, even_gla_norm_g, even_conv_w, even_conv_b, even_conv_norm_g, even_conv_norm_b,
           even_w_out, odd_w_in, odd_q_norm_g, odd_k_norm_g, odd_w_out, moe_w_group, moe_b_group,
           moe_w_router, moe_b_router, moe_w_gate, moe_w_up, moe_w_down, final_norm_g):
    b = x.shape[0]
    n = b * S
    tp = PROJ_TILE

    cvec = jnp.zeros((40, D), F32).at[0:b].set(c).at[b].set(c_ctx)
    mods = _mods(cvec, ada_w, ada_b)
    mod_lat = mods[:, 0:b].reshape(DEPTH, b, 1, 6, D)
    mod_ctx = jnp.broadcast_to(mods[:, b].reshape(DEPTH, 1, 1, 6, D), (DEPTH, b, 1, 6, D))
    mod_all = jnp.concatenate([mod_ctx, mod_lat], axis=2)

    xc = jnp.concatenate([ctx, x], axis=1).reshape(n, D)
    cos_t, sin_t = _rope_tables()
    fg = final_norm_g.reshape(1, D)
    xs_buf = jnp.zeros((n + MOE_CLASSES * MOE_BLOCK, ROUTE_W), F32)

    for l in range(DEPTH):
        i = l // 2
        mod = mod_all[l]
        w2, bias = _router_weights(moe_w_group[l], moe_b_group[l], moe_w_router[l], moe_b_router[l])
        if l % 2 == 0:
            w, wdec, bdec = _even_weights(even_w_in[i], even_dec_w_f[i], even_dec_b_f[i],
                                          even_dec_w_b[i], even_dec_b_b[i])
            p = _inproj(xc, mod, w, EVEN_W, tp)
            og = _gla(p, wdec, bdec, even_gla_norm_g[i].reshape(1, GLA_DV))
            cw8 = jnp.broadcast_to(even_conv_w[i].reshape(CONV_WIDTH, 1, CONV_CH), (CONV_WIDTH, 8, CONV_CH))
            oc = _conv(p, cw8, even_conv_b[i].reshape(1, -1), even_conv_norm_g[i].reshape(1, -1),
                       even_conv_norm_b[i].reshape(1, -1))
            xc, *routed = _outproj_router(og, 0, oc, 0, xc, mod, even_w_out[i].astype(BF16), w2, bias, tp)
        else:
            w, gain = _odd_weights(odd_w_in[i], odd_q_norm_g[i], odd_k_norm_g[i])
            p = _inproj(xc, mod, w, ODD_W, QK_PROJ_TILE, extra=(gain, cos_t, sin_t))
            oa = _attention(p, CTX)
            xc, *routed = _outproj_router(oa, 0, oa, 1, xc, mod, odd_w_out[i].astype(BF16), w2, bias, tp)
        xc, xs_buf = _moe(xc, routed, mod, (l, moe_w_gate, moe_w_up, moe_w_down), fg, xs_buf,
                          l == DEPTH - 1, MOE_BLOCK, DISPATCH_TILE, COMBINE_TILE)

    return xc.reshape(b, SEQ, D)
```
